```python
import jax, jax.numpy as jnp
from jax import lax
import numpy as np

D_MODEL = 2048
BATCH = 8
SEQ = 4096
DEPTH = 4

D_RWKV = D_MODEL // 2
D_POOL = D_MODEL - D_RWKV
HEAD_SIZE = 64
N_HEADS = D_RWKV // HEAD_SIZE
DECAY_LORA = 64
ICLR_LORA = 64
VRES_LORA = 32
POOL_WINDOWS = (2, 4, 8, 16)
N_POOL_GROUPS = len(POOL_WINDOWS)
POOL_GROUP_DIM = D_POOL // N_POOL_GROUPS
D_PLE = 256
N_SHIFT = 3 * D_RWKV + DECAY_LORA + ICLR_LORA
N_IN = N_SHIFT + D_RWKV + 2 * D_POOL
RMS_EPS = 1e-6
GN_EPS = 64e-5
KK_EPS = 1e-12

kernel_name = "hybrid_rwkv7_multiscale_pool_trunk"


def rms_norm(x, g):
    xf = x.astype(jnp.float32)
    y = xf * lax.rsqrt(jnp.mean(xf * xf, axis=-1, keepdims=True) + RMS_EPS)
    return (y * g.astype(jnp.float32)).astype(x.dtype)


def token_shift(s):
    return jnp.pad(s, ((0, 0), (1, 0), (0, 0)))[:, :-1]


def wkv7_scan(r, w, k, v, kk, a):
    B, T, H, N = r.shape

    def step(S, inp):
        r_t, w_t, k_t, v_t, kk_t, a_t = inp
        sa = jnp.einsum('bhvk,bhk->bhv', S, -kk_t)
        S = (S * w_t[:, :, None, :]
             + sa[..., None] * (kk_t * a_t)[:, :, None, :]
             + v_t[..., None] * k_t[:, :, None, :])
        y_t = jnp.einsum('bhvk,bhk->bhv', S, r_t)
        return S, y_t

    xs = tuple(jnp.moveaxis(t, 1, 0) for t in (r, w, k, v, kk, a))
    S0 = jnp.zeros((B, H, N, N), jnp.float32)
    _, ys = lax.scan(step, S0, xs)
    return jnp.moveaxis(ys, 0, 1)


def head_group_norm(y, w, b):
    B, T, H, N = y.shape
    mean = jnp.mean(y, axis=-1, keepdims=True)
    var = jnp.mean(jnp.square(y - mean), axis=-1, keepdims=True)
    yn = (y - mean) * lax.rsqrt(var + GN_EPS)
    return yn.reshape(B, T, H * N) * w.astype(jnp.float32) + b.astype(jnp.float32)


def causal_multiscale_pool(u):
    B, T, _ = u.shape
    ug = u.astype(jnp.float32).reshape(B, T, N_POOL_GROUPS, POOL_GROUP_DIM)
    c = jnp.cumsum(ug, axis=1)
    pos = jnp.arange(1, T + 1, dtype=jnp.float32)
    means = []
    for g, win in enumerate(POOL_WINDOWS):
        cg = c[:, :, g]
        prev = jnp.pad(cg, ((0, 0), (win, 0), (0, 0)))[:, :T]
        cnt = jnp.minimum(pos, jnp.float32(win))
        means.append((cg - prev) / cnt[None, :, None])
    mean = jnp.stack(means, axis=2)
    return mean - ug


def setup_inputs(seed: int = 0) -> dict:
    key = jax.random.key(seed)
    ks = jax.random.split(key, 24)
    f32 = jnp.float32
    nrm = lambda k, s, sc: jax.random.normal(k, s, f32) * sc
    L = DEPTH
    return {
        "x": nrm(ks[0], (BATCH, SEQ, D_MODEL), 1.0),
        "p": nrm(ks[1], (DEPTH, BATCH, SEQ, D_PLE), 1.0),
        "norm_g": 1.0 + nrm(ks[2], (L, D_MODEL), 0.02),
        "w_in": nrm(ks[3], (L, D_MODEL, N_IN), D_MODEL ** -0.5),
        "mu": jax.random.uniform(ks[4], (L, N_SHIFT), f32),
        "w0": jax.random.uniform(ks[5], (L, D_RWKV), f32, minval=-6.0, maxval=1.0),
        "w_up": nrm(ks[6], (L, DECAY_LORA, D_RWKV), 0.1 * DECAY_LORA ** -0.5),
        "a0": nrm(ks[7], (L, D_RWKV), 0.1),
        "a_up": nrm(ks[8], (L, ICLR_LORA, D_RWKV), 0.1 * ICLR_LORA ** -0.5),
        "v0": nrm(ks[9], (L - 1, D_RWKV), 0.1),
        "v_down": nrm(ks[10], (L - 1, D_RWKV, VRES_LORA), D_RWKV ** -0.5),
        "v_up": nrm(ks[11], (L - 1, VRES_LORA, D_RWKV), 0.5 * VRES_LORA ** -0.5),
        "k_k": 0.85 + nrm(ks[12], (L, D_RWKV), 0.02),
        "k_a": 1.0 + nrm(ks[13], (L, D_RWKV), 0.02),
        "r_k": nrm(ks[14], (L, N_HEADS, HEAD_SIZE), 0.1),
        "ln_w": 1.0 + nrm(ks[15], (L, D_RWKV), 0.02),
        "ln_b": nrm(ks[16], (L, D_RWKV), 0.01),
        "w_pool": nrm(ks[17], (L, N_POOL_GROUPS, POOL_GROUP_DIM, POOL_GROUP_DIM), POOL_GROUP_DIM ** -0.5),
        "pool_scale": 0.5 + nrm(ks[18], (L, D_POOL), 0.05),
        "w_out": nrm(ks[19], (L, D_MODEL, D_MODEL), D_MODEL ** -0.5),
        "w_ple": nrm(ks[20], (L, D_PLE, D_MODEL), D_PLE ** -0.5),
        "w_pg": nrm(ks[21], (L, D_MODEL, D_MODEL), D_MODEL ** -0.5),
        "b_pg": nrm(ks[22], (L, D_MODEL), 0.01),
        "final_g": 1.0 + nrm(ks[23], (D_MODEL,), 0.02),
    }


def reference(x, p, norm_g, w_in, mu, w0, w_up, a0, a_up, v0, v_down, v_up, k_k, k_a, r_k,
              ln_w, ln_b, w_pool, pool_scale, w_out, w_ple, w_pg, b_pg, final_g):
    B, T, _ = x.shape
    dt = x.dtype
    f32 = jnp.float32
    v_first = None
    o1 = D_RWKV
    o2 = 2 * D_RWKV
    o3 = 3 * D_RWKV
    o4 = o3 + DECAY_LORA
    for i in range(DEPTH):
        h = rms_norm(x, norm_g[i])
        z = jnp.einsum('btd,dn->btn', h, w_in[i])
        zs = z[..., :N_SHIFT]
        g_rwkv = z[..., N_SHIFT:N_SHIFT + D_RWKV]
        u = z[..., N_SHIFT + D_RWKV:N_SHIFT + D_RWKV + D_POOL]
        g_pool = z[..., N_SHIFT + D_RWKV + D_POOL:]

        zs = zs + (token_shift(zs) - zs) * mu[i]
        r = zs[..., :o1]
        k = zs[..., o1:o2]
        v = zs[..., o2:o3]
        wd = zs[..., o3:o4]
        ad = zs[..., o4:]
        w_loglog = -jax.nn.softplus(-(w0[i] + jnp.tanh(wd) @ w_up[i])) - 0.5
        decay = jnp.exp(-jnp.exp(w_loglog.astype(f32)))
        a = jax.nn.sigmoid(a0[i] + ad @ a_up[i])
        if i == 0:
            v_first = v
        else:
            nu = jax.nn.sigmoid(v0[i - 1] + (v @ v_down[i - 1]) @ v_up[i - 1])
            v = v + (v_first - v) * nu
        kk = (k * k_k[i]).astype(f32).reshape(B, T, N_HEADS, HEAD_SIZE)
        kk = kk / jnp.maximum(jnp.linalg.norm(kk, axis=-1, keepdims=True), KK_EPS)
        k = k * (1.0 + (a - 1.0) * k_a[i])
        rh = r.astype(f32).reshape(B, T, N_HEADS, HEAD_SIZE)
        kh = k.astype(f32).reshape(B, T, N_HEADS, HEAD_SIZE)
        vh = v.astype(f32).reshape(B, T, N_HEADS, HEAD_SIZE)
        ah = a.astype(f32).reshape(B, T, N_HEADS, HEAD_SIZE)
        wh = decay.reshape(B, T, N_HEADS, HEAD_SIZE)
        y = wkv7_scan(rh, wh, kh, vh, kk, ah)
        y = head_group_norm(y, ln_w[i], ln_b[i])
        bonus = jnp.sum(rh * kh * r_k[i].astype(f32), axis=-1, keepdims=True) * vh
        y_rwkv = (y + bonus.reshape(B, T, D_RWKV)).astype(dt) * jax.nn.silu(g_rwkv)

        d = causal_multiscale_pool(u).astype(dt)
        y_pool = jnp.einsum('btgc,gcd->btgd', d, w_pool[i]).reshape(B, T, D_POOL)
        y_pool = y_pool * pool_scale[i] * jax.nn.silu(g_pool)

        y_mix = jnp.concatenate([y_rwkv, y_pool], axis=-1)
        x = x + jnp.einsum('btc,cd->btd', y_mix, w_out[i])

        gate = jax.nn.sigmoid(jnp.einsum('btd,de->bte', x, w_pg[i]) + b_pg[i])
        x = x + gate * jnp.einsum('btq,qd->btd', p[i], w_ple[i])

    return rms_norm(x, final_g)
```

```python
import functools
import math

import jax
import jax.numpy as jnp
import numpy as np
from jax import lax
from jax.experimental import pallas as pl
from jax.experimental.pallas import tpu as pltpu

F32 = jnp.float32
BF16 = jnp.bfloat16

RMS_EPS = 1e-6
GN_EPS = 64e-5
KK_EPS = 1e-12
POOL_WINDOWS = (2, 4, 8, 16)

V7X_LANES = 128
V7X_VMEM_LIMIT_BYTES = 56 * 2**20
CHUNK = 64
MAX_POOL_WINDOW = max(POOL_WINDOWS)


def _dot(a, b):
    return jnp.dot(a, b, preferred_element_type=F32)


def _dot_nt(a, b):
    return lax.dot_general(a, b, (((1,), (1,)), ((), ())), preferred_element_type=F32)


def _dot_tn(a, b):
    return lax.dot_general(a, b, (((0,), (0,)), ((), ())), preferred_element_type=F32)


def _sigmoid(x):
    return 1.0 / (1.0 + jnp.exp(-x))


def _softplus(x):
    return jnp.maximum(x, 0.0) + jnp.log(1.0 + jnp.exp(-jnp.abs(x)))


def _split_bf16(x, parts):
    out = []
    for _ in range(parts):
        hi = x.astype(BF16)
        out.append(hi)
        x = x - hi.astype(F32)
    return out


def _dot_0_1(m01, x, parts):
    acc = None
    for part in _split_bf16(x, parts):
        d = _dot(m01, part)
        acc = d if acc is None else acc + d
    return acc


def _seg_sum(x, seg):
    acc = None
    for part in _split_bf16(x, 2):
        d = _dot(part, seg)
        acc = d if acc is None else acc + d
    return acc


def _in_proj_kernel(x_ref, g_ref, w_ref, o_ref, h_scr):
    @pl.when(pl.program_id(1) == 0)
    def _():
        x = x_ref[...]
        ms = jnp.mean(x * x, axis=-1, keepdims=True)
        h_scr[...] = (x * lax.rsqrt(ms + RMS_EPS) * g_ref[...]).astype(BF16)

    o_ref[...] = _dot(h_scr[...], w_ref[...])


def _col_tile(n, cap):
    best = V7X_LANES
    for k in range(1, n // V7X_LANES + 1):
        t = k * V7X_LANES
        if n % t == 0 and t <= cap:
            best = t
    return best


def _in_proj(x2, g, w, tm):
    m, d = x2.shape
    n = w.shape[1]
    tn = _col_tile(n, 1024)
    return pl.pallas_call(
        _in_proj_kernel,
        grid=(m // tm, n // tn),
        in_specs=[
            pl.BlockSpec((tm, d), lambda i, j: (i, 0)),
            pl.BlockSpec((1, d), lambda i, j: (0, 0)),
            pl.BlockSpec((d, tn), lambda i, j: (0, j)),
        ],
        out_specs=pl.BlockSpec((tm, tn), lambda i, j: (i, j)),
        out_shape=jax.ShapeDtypeStruct((m, n), F32),
        scratch_shapes=[pltpu.VMEM((tm, d), BF16)],
        compiler_params=pltpu.CompilerParams(
            dimension_semantics=("parallel", "arbitrary"), vmem_limit_bytes=V7X_VMEM_LIMIT_BYTES),
        name="in_proj",
    )(x2, g, w)


def _rwkv_kernel(*refs, first, n_pairs, head, d_lora):
    it = iter(refs)
    z_ref, zl_ref = next(it), next(it)
    vf_ref = None if first else next(it)
    mu_ref, mul_ref, w0_ref, wup_ref, a0_ref, aup_ref = (next(it) for _ in range(6))
    if not first:
        v0_ref, vdn_ref, vup_ref = next(it), next(it), next(it)
    kk_ref, ka_ref, rk_ref, lnw_ref, lnb_ref, seg_ref, tri_ref, amask_ref = (next(it) for _ in range(8))
    y_ref = next(it)
    vfo_ref = next(it) if first else None
    (carry_ref, carryl_ref, state_ref, at_s, rt_s, bt_s, kt_s, bh_s, kh_s, v_s, wc_s, y_s) = it

    tb = z_ref.shape[0]
    r_dim = w0_ref.shape[1]
    c = CHUNK
    n_chunks = tb // c
    levels = int(math.log2(c))

    @pl.when(pl.program_id(1) == 0)
    def _():
        carry_ref[...] = jnp.zeros_like(carry_ref)
        carryl_ref[...] = jnp.zeros_like(carryl_ref)
        state_ref[...] = jnp.zeros_like(state_ref)

    row = lax.broadcasted_iota(jnp.int32, (tb, 1), 0)

    def shift_mix(zv, carry_row, mu):
        prev = jnp.where(row == 0, carry_row, pltpu.roll(zv, 1, 0))
        return zv + (prev - zv) * mu

    r = shift_mix(z_ref[:, 0:r_dim], carry_ref[0:1, 0:r_dim], mu_ref[:, 0:r_dim])
    k = shift_mix(z_ref[:, r_dim:2 * r_dim], carry_ref[0:1, r_dim:2 * r_dim], mu_ref[:, r_dim:2 * r_dim])
    v = shift_mix(z_ref[:, 2 * r_dim:3 * r_dim], carry_ref[0:1, 2 * r_dim:3 * r_dim],
                  mu_ref[:, 2 * r_dim:3 * r_dim])
    lora_in = shift_mix(zl_ref[...], carryl_ref[0:1, :], mul_ref[...])
    carry_ref[0:1, :] = z_ref[tb - 1:tb, 0:3 * r_dim]
    carryl_ref[0:1, :] = zl_ref[tb - 1:tb, :]

    wd = lora_in[:, 0:d_lora]
    ad = lora_in[:, d_lora:]
    w_loglog = -_softplus(-(w0_ref[...] + _dot(jnp.tanh(wd).astype(BF16), wup_ref[...]))) - 0.5
    logw = -jnp.exp(w_loglog)
    a = _sigmoid(a0_ref[...] + _dot(ad.astype(BF16), aup_ref[...]))

    if first:
        vfo_ref[...] = v
    else:
        low = _dot(v.astype(BF16), vdn_ref[...])
        nu = _sigmoid(v0_ref[...] + _dot(low.astype(BF16), vup_ref[...]))
        v = v + (vf_ref[...] - v) * nu

    seg = seg_ref[...]
    kk = k * kk_ref[...]
    kk = kk / jnp.maximum(jnp.sqrt(_seg_sum(kk * kk, seg)), KK_EPS)
    k = k * (1.0 + (a - 1.0) * ka_ref[...])
    b = kk * a

    cc = _dot_0_1(tri_ref[...], logw, 3)
    cs = cc[0:tb]
    tot = cc[tb:2 * tb]
    w_inv = jnp.exp(-cs)
    w_rest = jnp.exp(tot - cs)
    at_s[...] = (-kk * jnp.exp(cs - logw)).astype(BF16)
    rt_s[...] = (r * jnp.exp(cs)).astype(BF16)
    bt_s[...] = (b * w_inv).astype(BF16)
    kt_s[...] = (k * w_inv).astype(BF16)
    bh_s[...] = (b * w_rest).astype(BF16)
    kh_s[...] = (k * w_rest).astype(BF16)
    v_s[...] = v.astype(BF16)
    wc_s[...] = jnp.exp(tot)

    bonus = _seg_sum(r * k * rk_ref[...], seg) * v
    g = z_ref[:, 3 * r_dim:4 * r_dim]
    gate = g * _sigmoid(g)

    lane = lax.broadcasted_iota(jnp.int32, (c, V7X_LANES), 1)
    lo_half = lane < head
    eye = (lax.broadcasted_iota(jnp.int32, (2 * c, 2 * c), 0)
           == lax.broadcasted_iota(jnp.int32, (2 * c, 2 * c), 1)).astype(F32)
    same_head = (lax.broadcasted_iota(jnp.int32, (V7X_LANES, V7X_LANES), 0) // head
                 == lax.broadcasted_iota(jnp.int32, (V7X_LANES, V7X_LANES), 1) // head)

    def stack(xv):
        zero = jnp.zeros_like(xv)
        return jnp.concatenate([jnp.where(lo_half, xv, zero), jnp.where(lo_half, zero, xv)], axis=0)

    def chunk_body(ci, carry):
        rows = pl.ds(pl.multiple_of(ci * c, c), c)
        for p in range(n_pairs):
            ls = slice(p * V7X_LANES, (p + 1) * V7X_LANES)
            lhs = jnp.concatenate([stack(at_s[rows, ls]), stack(rt_s[rows, ls])], axis=0)
            rhs = jnp.concatenate([stack(bt_s[rows, ls]), stack(kt_s[rows, ls])], axis=0)
            amat = jnp.where(amask_ref[...] != 0.0, _dot_nt(lhs, rhs), 0.0)
            nmat = amat[0:2 * c, 0:2 * c]
            a_ak = amat[0:2 * c, 2 * c:4 * c]
            a_r = amat[2 * c:4 * c, :]
            tinv = eye + nmat
            pw = nmat.astype(BF16)
            for _ in range(levels - 1):
                pw = _dot(pw, pw).astype(BF16)
                tinv = tinv + _dot(tinv.astype(BF16), pw)
            s0 = state_ref[p]
            xh = _dot_nt(lhs, s0.astype(BF16))
            vv = v_s[rows, ls]
            v2 = jnp.concatenate([vv, vv], axis=0)
            xm = xh[0:2 * c] + _dot(a_ak.astype(BF16), v2)
            u = _dot(tinv.astype(BF16), xm.astype(BF16))
            uv = jnp.concatenate([u.astype(BF16), v2], axis=0)
            y2 = xh[2 * c:4 * c] + _dot(a_r.astype(BF16), uv)
            y_s[rows, ls] = jnp.where(lo_half, y2[0:c], y2[c:2 * c])
            upd_rhs = jnp.concatenate([stack(bh_s[rows, ls]), stack(kh_s[rows, ls])], axis=0)
            upd = _dot_tn(uv, upd_rhs)
            wc = wc_s[pl.ds(pl.multiple_of(ci * c, c), 1), ls]
            state_ref[p] = s0 * wc + jnp.where(same_head, upd, 0.0)
        return carry

    lax.fori_loop(0, n_chunks, chunk_body, 0)

    y = y_s[...]
    inv_n = 1.0 / head
    mean = _seg_sum(y, seg) * inv_n
    yc = y - mean
    var = _seg_sum(yc * yc, seg) * inv_n
    yn = yc * lax.rsqrt(var + GN_EPS) * lnw_ref[...] + lnb_ref[...]
    y_ref[...] = (yn + bonus) * gate


def _rwkv_constants(tb, r_dim, head):
    c = CHUNK
    i = np.arange(tb)
    same_chunk = (i[:, None] // c) == (i[None, :] // c)
    tri = np.concatenate([same_chunk & (i[None, :] <= i[:, None]), same_chunk], axis=0)
    j = np.arange(4 * c)
    t_in = j[:, None] % c
    s_in = j[None, :] % c
    amask = np.where(j[:, None] < 2 * c, s_in < t_in, s_in <= t_in)
    lanes = jnp.arange(r_dim) // head
    seg = (lanes[:, None] == lanes[None, :]).astype(BF16)
    return seg, jnp.asarray(tri, BF16), jnp.asarray(amask, F32)


def _rwkv(z, vfirst, prm, *, batch, seq, tb, r_dim, pool_dim, heads, head, first):
    m = batch * seq
    nt = seq // tb
    d_lora = prm["w_up"].shape[0]
    l_dim = d_lora + prm["a_up"].shape[0]
    n_pairs = heads // 2
    seg, tri, amask = _rwkv_constants(tb, r_dim, head)

    def rows(width, col):
        return pl.BlockSpec((tb, width), lambda bi, ti: (bi * nt + ti, col))

    def whole(arr):
        return pl.BlockSpec(arr.shape, lambda bi, ti: (0,) * arr.ndim, pipeline_mode=pl.Buffered(1))

    ops = [z, z]
    specs = [rows(4 * r_dim, 0), rows(l_dim, (4 * r_dim + 2 * pool_dim) // l_dim)]
    if not first:
        ops.append(vfirst)
        specs.append(rows(r_dim, 0))
    names = ["mu_rkv", "mu_l", "w0", "w_up", "a0", "a_up"]
    if not first:
        names += ["v0", "v_down", "v_up"]
    names += ["k_k", "k_a", "r_k", "ln_w", "ln_b"]
    consts = [prm[nm] for nm in names] + [seg, tri, amask]
    ops += consts
    specs += [whole(a) for a in consts]

    out_shape = [jax.ShapeDtypeStruct((m, r_dim), F32)]
    out_specs = [rows(r_dim, 0)]
    if first:
        out_shape.append(jax.ShapeDtypeStruct((m, r_dim), F32))
        out_specs.append(rows(r_dim, 0))

    scratch = [
        pltpu.VMEM((8, 3 * r_dim), F32),
        pltpu.VMEM((8, l_dim), F32),
        pltpu.VMEM((n_pairs, V7X_LANES, V7X_LANES), F32),
    ] + [pltpu.VMEM((tb, r_dim), BF16) for _ in range(7)] + [
        pltpu.VMEM((tb, r_dim), F32),
        pltpu.VMEM((tb, r_dim), F32),
    ]
    res = pl.pallas_call(
        functools.partial(_rwkv_kernel, first=first, n_pairs=n_pairs, head=head, d_lora=d_lora),
        grid=(batch, nt),
        in_specs=specs,
        out_specs=out_specs,
        out_shape=out_shape,
        scratch_shapes=scratch,
        compiler_params=pltpu.CompilerParams(
            dimension_semantics=("parallel", "arbitrary"), vmem_limit_bytes=V7X_VMEM_LIMIT_BYTES),
        name="rwkv_first" if first else "rwkv",
    )(*ops)
    return (res[0], res[1]) if first else (res[0], vfirst)


def _pool_kernel(z_ref, wp_ref, ps_ref, y_ref, carry_ref):
    tb = z_ref.shape[0]
    p_dim = ps_ref.shape[1]
    cg = p_dim // len(POOL_WINDOWS)
    ti = pl.program_id(1)

    @pl.when(ti == 0)
    def _():
        carry_ref[...] = jnp.zeros_like(carry_ref)

    u = z_ref[:, 0:p_dim]
    gp = z_ref[:, p_dim:2 * p_dim]
    ext = jnp.concatenate([carry_ref[...], u], axis=0)
    carry_ref[...] = z_ref[tb - MAX_POOL_WINDOW:tb, 0:p_dim]

    sums = {}
    s = ext
    span = 1
    while span < MAX_POOL_WINDOW:
        s = s + pltpu.roll(s, span, 0)
        span *= 2
        sums[span] = s

    pos = (ti * tb + 1 + lax.broadcasted_iota(jnp.int32, (tb, 1), 0)).astype(F32)
    outs = []
    for gi, win in enumerate(POOL_WINDOWS):
        cols = slice(gi * cg, (gi + 1) * cg)
        mean = sums[win][MAX_POOL_WINDOW:, cols] / jnp.minimum(pos, float(win))
        d = mean - u[:, cols]
        outs.append(_dot(d.astype(BF16), wp_ref[gi]))
    y = jnp.concatenate(outs, axis=1)
    y_ref[...] = y * ps_ref[...] * (gp * _sigmoid(gp))


def _pool(z, w_pool, pool_scale, *, batch, seq, tb, r_dim, pool_dim):
    nt = seq // tb
    return pl.pallas_call(
        _pool_kernel,
        grid=(batch, nt),
        in_specs=[
            pl.BlockSpec((tb, 2 * pool_dim), lambda bi, ti: (bi * nt + ti, (4 * r_dim) // (2 * pool_dim))),
            pl.BlockSpec(w_pool.shape, lambda bi, ti: (0, 0, 0)),
            pl.BlockSpec(pool_scale.shape, lambda bi, ti: (0, 0)),
        ],
        out_specs=pl.BlockSpec((tb, pool_dim), lambda bi, ti: (bi * nt + ti, 0)),
        out_shape=jax.ShapeDtypeStruct((batch * seq, pool_dim), F32),
        scratch_shapes=[pltpu.VMEM((MAX_POOL_WINDOW, pool_dim), F32)],
        compiler_params=pltpu.CompilerParams(
            dimension_semantics=("parallel", "arbitrary"), vmem_limit_bytes=V7X_VMEM_LIMIT_BYTES),
        name="pool",
    )(z, w_pool, pool_scale)


def _out_kernel(x_ref, yr_ref, yp_ref, p_ref, wo_ref, wpg_ref, bpg_ref, wple_ref, fg_ref, o_ref, *, last):
    r_dim = yr_ref.shape[1]
    x1 = (x_ref[...]
          + _dot(yr_ref[...].astype(BF16), wo_ref[0:r_dim, :])
          + _dot(yp_ref[...].astype(BF16), wo_ref[r_dim:, :]))
    gate = _sigmoid(_dot(x1.astype(BF16), wpg_ref[...]) + bpg_ref[...])
    x2 = x1 + gate * _dot(p_ref[...].astype(BF16), wple_ref[...])
    if last:
        ms = jnp.mean(x2 * x2, axis=-1, keepdims=True)
        x2 = x2 * lax.rsqrt(ms + RMS_EPS) * fg_ref[...]
    o_ref[...] = x2


def _out(x2, yr, yp, p3, layer, w_out, w_pg, b_pg, w_ple, final_g, *, tm, last):
    m, d = x2.shape

    def whole(arr):
        return pl.BlockSpec(arr.shape, lambda i: (0,) * arr.ndim, pipeline_mode=pl.Buffered(1))

    return pl.pallas_call(
        functools.partial(_out_kernel, last=last),
        grid=(m // tm,),
        in_specs=[
            pl.BlockSpec((tm, d), lambda i: (i, 0)),
            pl.BlockSpec((tm, yr.shape[1]), lambda i: (i, 0)),
            pl.BlockSpec((tm, yp.shape[1]), lambda i: (i, 0)),
            pl.BlockSpec((None, tm, p3.shape[2]), lambda i: (layer, i, 0)),
            whole(w_out), whole(w_pg), whole(b_pg), whole(w_ple), whole(final_g),
        ],
        out_specs=pl.BlockSpec((tm, d), lambda i: (i, 0)),
        out_shape=jax.ShapeDtypeStruct((m, d), F32),
        compiler_params=pltpu.CompilerParams(
            dimension_semantics=("parallel",), vmem_limit_bytes=V7X_VMEM_LIMIT_BYTES),
        name="out_proj",
    )(x2, yr, yp, p3, w_out, w_pg, b_pg, w_ple, final_g)


def _tile(n, cap):
    t = min(n, cap)
    assert n % t == 0
    return t


def kernel(x, p, norm_g, w_in, mu, w0, w_up, a0, a_up, v0, v_down, v_up, k_k, k_a, r_k, ln_w, ln_b, w_pool,
           pool_scale, w_out, w_ple, w_pg, b_pg, final_g):
    batch, seq, d = x.shape
    depth = w_in.shape[0]
    r_dim = w0.shape[1]
    heads, head = r_k.shape[1], r_k.shape[2]
    pool_dim = pool_scale.shape[1]
    n_shift = mu.shape[1]
    l_dim = n_shift - 3 * r_dim
    m = batch * seq
    assert 2 * head == V7X_LANES and heads % 2 == 0 and heads * head == r_dim
    assert l_dim % V7X_LANES == 0 and (4 * r_dim + 2 * pool_dim) % l_dim == 0
    assert (4 * r_dim) % (2 * pool_dim) == 0 and pool_dim % len(POOL_WINDOWS) == 0

    tb_rwkv = _tile(seq, 256)
    tb_pool = _tile(seq, 512)
    tm_in = _tile(m, 512)
    tm_out = _tile(m, 256)
    assert tb_rwkv % CHUNK == 0 and tb_pool >= MAX_POOL_WINDOW

    w_in_p = jnp.concatenate(
        [w_in[..., :3 * r_dim], w_in[..., n_shift:n_shift + r_dim], w_in[..., n_shift + r_dim:],
         w_in[..., 3 * r_dim:n_shift]], axis=-1).astype(BF16)
    w_out_b = w_out.astype(BF16)
    w_pg_b = w_pg.astype(BF16)
    w_ple_b = w_ple.astype(BF16)
    w_pool_b = w_pool.astype(BF16)
    p3 = p.reshape(depth, m, p.shape[-1])
    x2 = x.reshape(m, d)
    fg = final_g.reshape(1, d)

    vfirst = None
    for i in range(depth):
        z = _in_proj(x2, norm_g[i].reshape(1, d), w_in_p[i], tm_in)
        prm = {
            "mu_rkv": mu[i, :3 * r_dim].reshape(1, -1), "mu_l": mu[i, 3 * r_dim:].reshape(1, -1),
            "w0": w0[i].reshape(1, -1), "w_up": w_up[i].astype(BF16),
            "a0": a0[i].reshape(1, -1), "a_up": a_up[i].astype(BF16),
            "k_k": k_k[i].reshape(1, -1), "k_a": k_a[i].reshape(1, -1), "r_k": r_k[i].reshape(1, -1),
            "ln_w": ln_w[i].reshape(1, -1), "ln_b": ln_b[i].reshape(1, -1),
        }
        if i > 0:
            prm.update({"v0": v0[i - 1].reshape(1, -1), "v_down": v_down[i - 1].astype(BF16),
                        "v_up": v_up[i - 1].astype(BF16)})
        yr, vfirst = _rwkv(z, vfirst, prm, batch=batch, seq=seq, tb=tb_rwkv, r_dim=r_dim,
                           pool_dim=pool_dim, heads=heads, head=head, first=(i == 0))
        yp = _pool(z, w_pool_b[i], pool_scale[i].reshape(1, -1), batch=batch, seq=seq, tb=tb_pool,
                   r_dim=r_dim, pool_dim=pool_dim)
        x2 = _out(x2, yr, yp, p3, i, w_out_b[i], w_pg_b[i], b_pg[i].reshape(1, d), w_ple_b[i], fg,
                  tm=tm_out, last=(i == depth - 1))
    return x2.reshape(batch, seq, d)
```

```python
import functools
import math

import jax
import jax.numpy as jnp
import numpy as np
from jax import lax
from jax.experimental import pallas as pl
from jax.experimental.pallas import tpu as pltpu

F32 = jnp.float32
BF16 = jnp.bfloat16

RMS_EPS = 1e-6
GN_EPS = 64e-5
KK_EPS = 1e-12
POOL_WINDOWS = (2, 4, 8, 16)

V7X_LANES = 128
V7X_MXU_DIM = 256
V7X_VMEM_LIMIT_BYTES = 56 * 2**20
CHUNK = 64
MAX_POOL_WINDOW = max(POOL_WINDOWS)


def _dot(a, b):
    return jnp.dot(a, b, preferred_element_type=F32)


def _dot_nt(a, b):
    return lax.dot_general(a, b, (((1,), (1,)), ((), ())), preferred_element_type=F32)


def _dot_tn(a, b):
    return lax.dot_general(a, b, (((0,), (0,)), ((), ())), preferred_element_type=F32)


def _sigmoid(x):
    return 1.0 / (1.0 + jnp.exp(-x))


def _softplus(x):
    return jnp.maximum(x, 0.0) + jnp.log(1.0 + jnp.exp(-jnp.abs(x)))


def _split_bf16(x, parts):
    out = []
    for _ in range(parts):
        hi = x.astype(BF16)
        out.append(hi)
        x = x - hi.astype(F32)
    return out


def _dot_0_1(m01, x, parts):
    acc = None
    for part in _split_bf16(x, parts):
        d = _dot(m01, part)
        acc = d if acc is None else acc + d
    return acc


def _seg_sum(x, seg):
    w = seg.shape[0]
    xb = x.astype(BF16)
    return jnp.concatenate(
        [_dot(xb[:, q * w:(q + 1) * w], seg) for q in range(x.shape[1] // w)], axis=1)


def _in_proj_kernel(x_ref, g_ref, w_ref, o_ref, h_scr):
    @pl.when(pl.program_id(1) == 0)
    def _():
        x = x_ref[...]
        ms = jnp.mean(x * x, axis=-1, keepdims=True)
        h_scr[...] = (x * lax.rsqrt(ms + RMS_EPS) * g_ref[...]).astype(BF16)

    o_ref[...] = _dot(h_scr[...], w_ref[...])


def _col_tile(n, cap):
    best = V7X_LANES
    for k in range(1, n // V7X_LANES + 1):
        t = k * V7X_LANES
        if n % t == 0 and t <= cap:
            best = t
    return best


def _in_proj(x2, g, w, tm):
    m, d = x2.shape
    n = w.shape[1]
    tn = _col_tile(n, 1024)
    return pl.pallas_call(
        _in_proj_kernel,
        grid=(m // tm, n // tn),
        in_specs=[
            pl.BlockSpec((tm, d), lambda i, j: (i, 0)),
            pl.BlockSpec((1, d), lambda i, j: (0, 0)),
            pl.BlockSpec((d, tn), lambda i, j: (0, j)),
        ],
        out_specs=pl.BlockSpec((tm, tn), lambda i, j: (i, j)),
        out_shape=jax.ShapeDtypeStruct((m, n), F32),
        scratch_shapes=[pltpu.VMEM((tm, d), BF16)],
        compiler_params=pltpu.CompilerParams(
            dimension_semantics=("parallel", "arbitrary"), vmem_limit_bytes=V7X_VMEM_LIMIT_BYTES),
        name="in_proj",
    )(x2, g, w)


def _rwkv_kernel(*refs, first, n_pairs, head, d_lora):
    it = iter(refs)
    z_ref, zl_ref = next(it), next(it)
    vf_ref = None if first else next(it)
    mu_ref, mul_ref, w0_ref, wup_ref, a0_ref, aup_ref = (next(it) for _ in range(6))
    if not first:
        v0_ref, vdn_ref, vup_ref = next(it), next(it), next(it)
    kk_ref, ka_ref, rk_ref, lnw_ref, lnb_ref, seg_ref, tri_ref, amask_ref = (next(it) for _ in range(8))
    y_ref = next(it)
    vfo_ref = next(it) if first else None
    (carry_ref, carryl_ref, state_ref, at_s, rt_s, bt_s, kt_s, bh_s, kh_s, v_s, wc_s, y_s,
     pw_s, tinv_s, aak_s, ar_s, xloc_s, ta_s, uloc_s, u_s, yr_s) = it

    tb = z_ref.shape[0]
    r_dim = w0_ref.shape[1]
    c = CHUNK
    n_chunks = tb // c
    levels = int(math.log2(c))

    @pl.when(pl.program_id(1) == 0)
    def _():
        carry_ref[...] = jnp.zeros_like(carry_ref)
        carryl_ref[...] = jnp.zeros_like(carryl_ref)
        state_ref[...] = jnp.zeros_like(state_ref)

    row = lax.broadcasted_iota(jnp.int32, (tb, 1), 0)

    def shift_mix(zv, carry_row, mu):
        prev = jnp.where(row == 0, carry_row, pltpu.roll(zv, 1, 0))
        return zv + (prev - zv) * mu

    r = shift_mix(z_ref[:, 0:r_dim], carry_ref[0:1, 0:r_dim], mu_ref[:, 0:r_dim])
    k = shift_mix(z_ref[:, r_dim:2 * r_dim], carry_ref[0:1, r_dim:2 * r_dim], mu_ref[:, r_dim:2 * r_dim])
    v = shift_mix(z_ref[:, 2 * r_dim:3 * r_dim], carry_ref[0:1, 2 * r_dim:3 * r_dim],
                  mu_ref[:, 2 * r_dim:3 * r_dim])
    lora_in = shift_mix(zl_ref[...], carryl_ref[0:1, :], mul_ref[...])
    carry_ref[0:1, :] = z_ref[tb - 1:tb, 0:3 * r_dim]
    carryl_ref[0:1, :] = zl_ref[tb - 1:tb, :]

    wd = lora_in[:, 0:d_lora]
    ad = lora_in[:, d_lora:]
    w_loglog = -_softplus(-(w0_ref[...] + _dot(jnp.tanh(wd).astype(BF16), wup_ref[...]))) - 0.5
    logw = -jnp.exp(w_loglog)
    a = _sigmoid(a0_ref[...] + _dot(ad.astype(BF16), aup_ref[...]))

    if first:
        vfo_ref[...] = v
    else:
        low = _dot(v.astype(BF16), vdn_ref[...])
        nu = _sigmoid(v0_ref[...] + _dot(low.astype(BF16), vup_ref[...]))
        v = v + (vf_ref[...] - v) * nu

    seg = seg_ref[...]
    kk = k * kk_ref[...]
    kk = kk * lax.rsqrt(jnp.maximum(_seg_sum(kk * kk, seg), KK_EPS * KK_EPS))
    k = k * (1.0 + (a - 1.0) * ka_ref[...])
    b = kk * a

    cs = _dot_0_1(tri_ref[...], logw, 2)
    last = [cs[(i + 1) * c - 1:(i + 1) * c] for i in range(n_chunks)]
    tot = jnp.concatenate([jnp.broadcast_to(t, (c, r_dim)) for t in last], axis=0)
    w_inv = jnp.exp(-cs)
    w_rest = jnp.exp(tot - cs)
    at_s[...] = (-kk * jnp.exp(cs - logw)).astype(BF16)
    rt_s[...] = (r * jnp.exp(cs)).astype(BF16)
    bt_s[...] = (b * w_inv).astype(BF16)
    kt_s[...] = (k * w_inv).astype(BF16)
    bh_s[...] = (b * w_rest).astype(BF16)
    kh_s[...] = (k * w_rest).astype(BF16)
    v_s[...] = v.astype(BF16)
    for i in range(n_chunks):
        wc_s[i:i + 1, :] = jnp.exp(last[i])

    bonus = _seg_sum(r * k * rk_ref[...], seg) * v
    g = z_ref[:, 3 * r_dim:4 * r_dim]
    gate = g * _sigmoid(g)

    lane = lax.broadcasted_iota(jnp.int32, (c, V7X_LANES), 1)
    lo_half = lane < head
    eye = (lax.broadcasted_iota(jnp.int32, (2 * c, 2 * c), 0)
           == lax.broadcasted_iota(jnp.int32, (2 * c, 2 * c), 1)).astype(F32)
    same_head = (lax.broadcasted_iota(jnp.int32, (V7X_LANES, V7X_LANES), 0) // head
                 == lax.broadcasted_iota(jnp.int32, (V7X_LANES, V7X_LANES), 1) // head)

    def stack(src, ci, p):
        xv = src[ci * c:(ci + 1) * c, p * V7X_LANES:(p + 1) * V7X_LANES]
        zero = jnp.zeros_like(xv)
        return jnp.concatenate([jnp.where(lo_half, xv, zero), jnp.where(lo_half, zero, xv)], axis=0)

    def twice(src, ci, p):
        xv = src[ci * c:(ci + 1) * c, p * V7X_LANES:(p + 1) * V7X_LANES]
        return jnp.concatenate([xv, xv], axis=0)

    problems = [(ci, p) for ci in range(n_chunks) for p in range(n_pairs)]

    for q, (ci, p) in enumerate(problems):
        lhs = jnp.concatenate([stack(at_s, ci, p), stack(rt_s, ci, p)], axis=0)
        rhs = jnp.concatenate([stack(bt_s, ci, p), stack(kt_s, ci, p)], axis=0)
        amat = jnp.where(amask_ref[...] != 0.0, _dot_nt(lhs, rhs), 0.0)
        nmat = amat[0:2 * c, 0:2 * c]
        pw_s[q] = nmat.astype(BF16)
        tinv_s[q] = eye + nmat
        aak_s[q] = amat[0:2 * c, 2 * c:4 * c].astype(BF16)
        ar_s[q] = amat[2 * c:4 * c, :].astype(BF16)
    for _ in range(levels - 1):
        for q in range(len(problems)):
            pw = pw_s[q]
            pw_s[q] = _dot(pw, pw).astype(BF16)
        for q in range(len(problems)):
            tinv = tinv_s[q]
            tinv_s[q] = tinv + _dot(tinv.astype(BF16), pw_s[q])
    for q, (ci, p) in enumerate(problems):
        xloc_s[q] = _dot(aak_s[q], twice(v_s, ci, p)).astype(BF16)
    for q, (ci, p) in enumerate(problems):
        tu = _dot(tinv_s[q].astype(BF16), jnp.concatenate([stack(at_s, ci, p), xloc_s[q]], axis=1))
        ta_s[q] = tu[:, 0:V7X_LANES].astype(BF16)
        uloc_s[q] = tu[:, V7X_LANES:2 * V7X_LANES]

    for ci in range(n_chunks):
        for p in range(n_pairs):
            q = ci * n_pairs + p
            lhs = jnp.concatenate([ta_s[q], stack(rt_s, ci, p)], axis=0)
            xh = _dot_nt(lhs, state_ref[p].astype(BF16))
            u_s[p] = (xh[0:2 * c] + uloc_s[q]).astype(BF16)
            yr_s[p] = xh[2 * c:4 * c]
        for p in range(n_pairs):
            q = ci * n_pairs + p
            ls = slice(p * V7X_LANES, (p + 1) * V7X_LANES)
            uv = jnp.concatenate([u_s[p], twice(v_s, ci, p)], axis=0)
            upd_rhs = jnp.concatenate([stack(bh_s, ci, p), stack(kh_s, ci, p)], axis=0)
            upd = _dot_tn(uv, upd_rhs)
            state_ref[p] = state_ref[p] * wc_s[ci:ci + 1, ls] + jnp.where(same_head, upd, 0.0)
            y2 = yr_s[p] + _dot(ar_s[q], uv)
            y_s[ci * c:(ci + 1) * c, ls] = jnp.where(lo_half, y2[0:c], y2[c:2 * c])

    y = y_s[...]
    inv_n = 1.0 / head
    mean = _seg_sum(y, seg) * inv_n
    yc = y - mean
    var = _seg_sum(yc * yc, seg) * inv_n
    yn = yc * lax.rsqrt(var + GN_EPS) * lnw_ref[...] + lnb_ref[...]
    y_ref[...] = (yn + bonus) * gate


def _rwkv_constants(tb, head):
    c = CHUNK
    i = np.arange(tb)
    tri = ((i[:, None] // c) == (i[None, :] // c)) & (i[None, :] <= i[:, None])
    j = np.arange(4 * c)
    t_in = j[:, None] % c
    s_in = j[None, :] % c
    amask = np.where(j[:, None] < 2 * c, s_in < t_in, s_in <= t_in)
    lanes = np.arange(V7X_MXU_DIM) // head
    seg = lanes[:, None] == lanes[None, :]
    return jnp.asarray(seg, BF16), jnp.asarray(tri, BF16), jnp.asarray(amask, F32)


def _rwkv(z, vfirst, prm, *, batch, seq, tb, r_dim, pool_dim, heads, head, first):
    m = batch * seq
    nt = seq // tb
    d_lora = prm["w_up"].shape[0]
    l_dim = d_lora + prm["a_up"].shape[0]
    n_pairs = heads // 2
    seg, tri, amask = _rwkv_constants(tb, head)
    n_prob = (tb // CHUNK) * n_pairs
    c2 = 2 * CHUNK

    def rows(width, col):
        return pl.BlockSpec((tb, width), lambda bi, ti: (bi * nt + ti, col))

    def whole(arr):
        return pl.BlockSpec(arr.shape, lambda bi, ti: (0,) * arr.ndim, pipeline_mode=pl.Buffered(1))

    ops = [z, z]
    specs = [rows(4 * r_dim, 0), rows(l_dim, (4 * r_dim + 2 * pool_dim) // l_dim)]
    if not first:
        ops.append(vfirst)
        specs.append(rows(r_dim, 0))
    names = ["mu_rkv", "mu_l", "w0", "w_up", "a0", "a_up"]
    if not first:
        names += ["v0", "v_down", "v_up"]
    names += ["k_k", "k_a", "r_k", "ln_w", "ln_b"]
    consts = [prm[nm] for nm in names] + [seg, tri, amask]
    ops += consts
    specs += [whole(a) for a in consts]

    out_shape = [jax.ShapeDtypeStruct((m, r_dim), F32)]
    out_specs = [rows(r_dim, 0)]
    if first:
        out_shape.append(jax.ShapeDtypeStruct((m, r_dim), F32))
        out_specs.append(rows(r_dim, 0))

    scratch = [
        pltpu.VMEM((8, 3 * r_dim), F32),
        pltpu.VMEM((8, l_dim), F32),
        pltpu.VMEM((n_pairs, V7X_LANES, V7X_LANES), F32),
    ] + [pltpu.VMEM((tb, r_dim), BF16) for _ in range(7)] + [
        pltpu.VMEM((8, r_dim), F32),
        pltpu.VMEM((tb, r_dim), F32),
        pltpu.VMEM((n_prob, c2, c2), BF16),
        pltpu.VMEM((n_prob, c2, c2), F32),
        pltpu.VMEM((n_prob, c2, c2), BF16),
        pltpu.VMEM((n_prob, c2, 2 * c2), BF16),
        pltpu.VMEM((n_prob, c2, V7X_LANES), BF16),
        pltpu.VMEM((n_prob, c2, V7X_LANES), BF16),
        pltpu.VMEM((n_prob, c2, V7X_LANES), F32),
        pltpu.VMEM((n_pairs, c2, V7X_LANES), BF16),
        pltpu.VMEM((n_pairs, c2, V7X_LANES), F32),
    ]
    res = pl.pallas_call(
        functools.partial(_rwkv_kernel, first=first, n_pairs=n_pairs, head=head, d_lora=d_lora),
        grid=(batch, nt),
        in_specs=specs,
        out_specs=out_specs,
        out_shape=out_shape,
        scratch_shapes=scratch,
        compiler_params=pltpu.CompilerParams(
            dimension_semantics=("parallel", "arbitrary"), vmem_limit_bytes=V7X_VMEM_LIMIT_BYTES),
        name="rwkv_first" if first else "rwkv",
    )(*ops)
    return (res[0], res[1]) if first else (res[0], vfirst)


def _pool_kernel(z_ref, wp_ref, ps_ref, y_ref, carry_ref):
    tb = z_ref.shape[0]
    p_dim = ps_ref.shape[1]
    cg = p_dim // len(POOL_WINDOWS)
    ti = pl.program_id(1)

    @pl.when(ti == 0)
    def _():
        carry_ref[...] = jnp.zeros_like(carry_ref)

    u = z_ref[:, 0:p_dim]
    gp = z_ref[:, p_dim:2 * p_dim]
    ext = jnp.concatenate([carry_ref[...], u], axis=0)
    carry_ref[...] = z_ref[tb - MAX_POOL_WINDOW:tb, 0:p_dim]

    sums = {}
    s = ext
    span = 1
    while span < MAX_POOL_WINDOW:
        s = s + pltpu.roll(s, span, 0)
        span *= 2
        sums[span] = s

    pos = (ti * tb + 1 + lax.broadcasted_iota(jnp.int32, (tb, 1), 0)).astype(F32)
    outs = []
    for gi, win in enumerate(POOL_WINDOWS):
        cols = slice(gi * cg, (gi + 1) * cg)
        mean = sums[win][MAX_POOL_WINDOW:, cols] / jnp.minimum(pos, float(win))
        d = mean - u[:, cols]
        outs.append(_dot(d.astype(BF16), wp_ref[gi]))
    y = jnp.concatenate(outs, axis=1)
    y_ref[...] = y * ps_ref[...] * (gp * _sigmoid(gp))


def _pool(z, w_pool, pool_scale, *, batch, seq, tb, r_dim, pool_dim):
    nt = seq // tb
    return pl.pallas_call(
        _pool_kernel,
        grid=(batch, nt),
        in_specs=[
            pl.BlockSpec((tb, 2 * pool_dim), lambda bi, ti: (bi * nt + ti, (4 * r_dim) // (2 * pool_dim))),
            pl.BlockSpec(w_pool.shape, lambda bi, ti: (0, 0, 0)),
            pl.BlockSpec(pool_scale.shape, lambda bi, ti: (0, 0)),
        ],
        out_specs=pl.BlockSpec((tb, pool_dim), lambda bi, ti: (bi * nt + ti, 0)),
        out_shape=jax.ShapeDtypeStruct((batch * seq, pool_dim), F32),
        scratch_shapes=[pltpu.VMEM((MAX_POOL_WINDOW, pool_dim), F32)],
        compiler_params=pltpu.CompilerParams(
            dimension_semantics=("parallel", "arbitrary"), vmem_limit_bytes=V7X_VMEM_LIMIT_BYTES),
        name="pool",
    )(z, w_pool, pool_scale)


def _out_kernel(x_ref, yr_ref, yp_ref, p_ref, wo_ref, wpg_ref, bpg_ref, wple_ref, fg_ref, o_ref, *, last):
    r_dim = yr_ref.shape[1]
    x1 = (x_ref[...]
          + _dot(yr_ref[...].astype(BF16), wo_ref[0:r_dim, :])
          + _dot(yp_ref[...].astype(BF16), wo_ref[r_dim:, :]))
    gate = _sigmoid(_dot(x1.astype(BF16), wpg_ref[...]) + bpg_ref[...])
    x2 = x1 + gate * _dot(p_ref[...].astype(BF16), wple_ref[...])
    if last:
        ms = jnp.mean(x2 * x2, axis=-1, keepdims=True)
        x2 = x2 * lax.rsqrt(ms + RMS_EPS) * fg_ref[...]
    o_ref[...] = x2


def _out(x2, yr, yp, p3, layer, w_out, w_pg, b_pg, w_ple, final_g, *, tm, last):
    m, d = x2.shape

    def whole(arr):
        return pl.BlockSpec(arr.shape, lambda i: (0,) * arr.ndim, pipeline_mode=pl.Buffered(1))

    return pl.pallas_call(
        functools.partial(_out_kernel, last=last),
        grid=(m // tm,),
        in_specs=[
            pl.BlockSpec((tm, d), lambda i: (i, 0)),
            pl.BlockSpec((tm, yr.shape[1]), lambda i: (i, 0)),
            pl.BlockSpec((tm, yp.shape[1]), lambda i: (i, 0)),
            pl.BlockSpec((None, tm, p3.shape[2]), lambda i: (layer, i, 0)),
            whole(w_out), whole(w_pg), whole(b_pg), whole(w_ple), whole(final_g),
        ],
        out_specs=pl.BlockSpec((tm, d), lambda i: (i, 0)),
        out_shape=jax.ShapeDtypeStruct((m, d), F32),
        compiler_params=pltpu.CompilerParams(
            dimension_semantics=("parallel",), vmem_limit_bytes=V7X_VMEM_LIMIT_BYTES),
        name="out_proj",
    )(x2, yr, yp, p3, w_out, w_pg, b_pg, w_ple, final_g)


def _tile(n, cap):
    t = min(n, cap)
    assert n % t == 0
    return t


def kernel(x, p, norm_g, w_in, mu, w0, w_up, a0, a_up, v0, v_down, v_up, k_k, k_a, r_k, ln_w, ln_b, w_pool,
           pool_scale, w_out, w_ple, w_pg, b_pg, final_g):
    batch, seq, d = x.shape
    depth = w_in.shape[0]
    r_dim = w0.shape[1]
    heads, head = r_k.shape[1], r_k.shape[2]
    pool_dim = pool_scale.shape[1]
    n_shift = mu.shape[1]
    l_dim = n_shift - 3 * r_dim
    m = batch * seq
    assert 2 * head == V7X_LANES and heads % 2 == 0 and heads * head == r_dim and r_dim % V7X_MXU_DIM == 0
    assert l_dim % V7X_LANES == 0 and (4 * r_dim + 2 * pool_dim) % l_dim == 0
    assert (4 * r_dim) % (2 * pool_dim) == 0 and pool_dim % len(POOL_WINDOWS) == 0

    tb_rwkv = _tile(seq, 256)
    tb_pool = _tile(seq, 512)
    tm_in = _tile(m, 512)
    tm_out = _tile(m, 256)
    assert tb_rwkv % CHUNK == 0 and tb_rwkv // CHUNK <= 8 and tb_pool >= MAX_POOL_WINDOW

    w_in_p = jnp.concatenate(
        [w_in[..., :3 * r_dim], w_in[..., n_shift:n_shift + r_dim], w_in[..., n_shift + r_dim:],
         w_in[..., 3 * r_dim:n_shift]], axis=-1).astype(BF16)
    w_out_b = w_out.astype(BF16)
    w_pg_b = w_pg.astype(BF16)
    w_ple_b = w_ple.astype(BF16)
    w_pool_b = w_pool.astype(BF16)
    p3 = p.reshape(depth, m, p.shape[-1])
    x2 = x.reshape(m, d)
    fg = final_g.reshape(1, d)

    vfirst = None
    for i in range(depth):
        z = _in_proj(x2, norm_g[i].reshape(1, d), w_in_p[i], tm_in)
        prm = {
            "mu_rkv": mu[i, :3 * r_dim].reshape(1, -1), "mu_l": mu[i, 3 * r_dim:].reshape(1, -1),
            "w0": w0[i].reshape(1, -1), "w_up": w_up[i].astype(BF16),
            "a0": a0[i].reshape(1, -1), "a_up": a_up[i].astype(BF16),
            "k_k": k_k[i].reshape(1, -1), "k_a": k_a[i].reshape(1, -1), "r_k": r_k[i].reshape(1, -1),
            "ln_w": ln_w[i].reshape(1, -1), "ln_b": ln_b[i].reshape(1, -1),
        }
        if i > 0:
            prm.update({"v0": v0[i - 1].reshape(1, -1), "v_down": v_down[i - 1].astype(BF16),
                        "v_up": v_up[i - 1].astype(BF16)})
        yr, vfirst = _rwkv(z, vfirst, prm, batch=batch, seq=seq, tb=tb_rwkv, r_dim=r_dim,
                           pool_dim=pool_dim, heads=heads, head=head, first=(i == 0))
        yp = _pool(z, w_pool_b[i], pool_scale[i].reshape(1, -1), batch=batch, seq=seq, tb=tb_pool,
                   r_dim=r_dim, pool_dim=pool_dim)
        x2 = _out(x2, yr, yp, p3, i, w_out_b[i], w_pg_b[i], b_pg[i].reshape(1, d), w_ple_b[i], fg,
                  tm=tm_out, last=(i == depth - 1))
    return x2.reshape(batch, seq, d)
```

```python
import functools
import math

import jax
import jax.numpy as jnp
import numpy as np
from jax import lax
from jax.experimental import pallas as pl
from jax.experimental.pallas import tpu as pltpu

F32 = jnp.float32
BF16 = jnp.bfloat16

RMS_EPS = 1e-6
GN_EPS = 64e-5
KK_EPS = 1e-12
POOL_WINDOWS = (2, 4, 8, 16)

V7X_LANES = 128
V7X_MXU_DIM = 256
V7X_VMEM_LIMIT_BYTES = 56 * 2**20
CHUNK = 64
MAX_POOL_WINDOW = max(POOL_WINDOWS)


def _dot(a, b):
    return jnp.dot(a, b, preferred_element_type=F32)


def _dot_nt(a, b):
    return lax.dot_general(a, b, (((1,), (1,)), ((), ())), preferred_element_type=F32)


def _dot_tn(a, b):
    return lax.dot_general(a, b, (((0,), (0,)), ((), ())), preferred_element_type=F32)


def _sigmoid(x):
    return 1.0 / (1.0 + jnp.exp(-x))


def _softplus(x):
    return jnp.maximum(x, 0.0) + jnp.log(1.0 + jnp.exp(-jnp.abs(x)))


def _split_bf16(x, parts):
    out = []
    for _ in range(parts):
        hi = x.astype(BF16)
        out.append(hi)
        x = x - hi.astype(F32)
    return out


def _dot_0_1(m01, x, parts):
    acc = None
    for part in _split_bf16(x, parts):
        d = _dot(m01, part)
        acc = d if acc is None else acc + d
    return acc


def _seg_sum(x, seg):
    w = seg.shape[0]
    xb = x.astype(BF16)
    return jnp.concatenate(
        [_dot(xb[:, q * w:(q + 1) * w], seg) for q in range(x.shape[1] // w)], axis=1)


def _in_proj_kernel(x_ref, g_ref, w_ref, wl_ref, o_ref, ol_ref, h_scr):
    @pl.when(pl.program_id(1) == 0)
    def _():
        x = x_ref[...]
        ms = jnp.mean(x * x, axis=-1, keepdims=True)
        h_scr[...] = (x * lax.rsqrt(ms + RMS_EPS) * g_ref[...]).astype(BF16)
        ol_ref[...] = _dot(h_scr[...], wl_ref[...])

    o_ref[...] = _dot(h_scr[...], w_ref[...])


def _col_tile(n, cap):
    best = n
    for k in range(1, n // V7X_MXU_DIM + 1):
        t = k * V7X_MXU_DIM
        if n % t == 0 and t <= cap:
            best = t
    return best


def _in_proj(x2, g, w, w_l, tm):
    m, d = x2.shape
    n = w.shape[1]
    n_l = w_l.shape[1]
    tn = _col_tile(n, 1024)
    return pl.pallas_call(
        _in_proj_kernel,
        grid=(m // tm, n // tn),
        in_specs=[
            pl.BlockSpec((tm, d), lambda i, j: (i, 0)),
            pl.BlockSpec((1, d), lambda i, j: (0, 0)),
            pl.BlockSpec((d, tn), lambda i, j: (0, j)),
            pl.BlockSpec((d, n_l), lambda i, j: (0, 0)),
        ],
        out_specs=[pl.BlockSpec((tm, tn), lambda i, j: (i, j)),
                   pl.BlockSpec((tm, n_l), lambda i, j: (i, 0))],
        out_shape=[jax.ShapeDtypeStruct((m, n), F32), jax.ShapeDtypeStruct((m, n_l), F32)],
        scratch_shapes=[pltpu.VMEM((tm, d), BF16)],
        compiler_params=pltpu.CompilerParams(
            dimension_semantics=("parallel", "arbitrary"), vmem_limit_bytes=V7X_VMEM_LIMIT_BYTES),
        name="in_proj",
    )(x2, g, w, w_l)


def _rwkv_kernel(*refs, first, n_pairs, head, d_lora):
    it = iter(refs)
    z_ref, zl_ref = next(it), next(it)
    vf_ref = None if first else next(it)
    mu_ref, mul_ref, w0_ref, wup_ref, a0_ref, aup_ref = (next(it) for _ in range(6))
    if not first:
        v0_ref, vdn_ref, vup_ref = next(it), next(it), next(it)
    kk_ref, ka_ref, rk_ref, lnw_ref, lnb_ref, seg_ref, segn_ref, tri_ref = (next(it) for _ in range(8))
    y_ref = next(it)
    vfo_ref = next(it) if first else None
    (carry_ref, carryl_ref, state_ref, at_s, rt_s, bt_s, kt_s, v_s, wc_s, y_s,
     ch_s, tinv_s, aak_s, ar_s, xloc_s, ta_s, uloc_s, u_s, yr_s) = it

    tb = z_ref.shape[0]
    r_dim = w0_ref.shape[1]
    c = CHUNK
    n_chunks = tb // c
    levels = int(math.log2(c))

    @pl.when(pl.program_id(1) == 0)
    def _():
        carry_ref[...] = jnp.zeros_like(carry_ref)
        carryl_ref[...] = jnp.zeros_like(carryl_ref)
        state_ref[...] = jnp.zeros_like(state_ref)

    row = lax.broadcasted_iota(jnp.int32, (tb, 1), 0)

    def shift_mix(zv, carry_row, mu):
        prev = jnp.where(row == 0, carry_row, pltpu.roll(zv, 1, 0))
        return zv + (prev - zv) * mu

    r = shift_mix(z_ref[:, 0:r_dim], carry_ref[0:1, 0:r_dim], mu_ref[:, 0:r_dim])
    k = shift_mix(z_ref[:, r_dim:2 * r_dim], carry_ref[0:1, r_dim:2 * r_dim], mu_ref[:, r_dim:2 * r_dim])
    v = shift_mix(z_ref[:, 2 * r_dim:3 * r_dim], carry_ref[0:1, 2 * r_dim:3 * r_dim],
                  mu_ref[:, 2 * r_dim:3 * r_dim])
    lora_in = shift_mix(zl_ref[...], carryl_ref[0:1, :], mul_ref[...])
    carry_ref[0:1, :] = z_ref[tb - 1:tb, 0:3 * r_dim]
    carryl_ref[0:1, :] = zl_ref[tb - 1:tb, :]

    wd = lora_in[:, 0:d_lora]
    ad = lora_in[:, d_lora:]
    w_loglog = -_softplus(-(w0_ref[...] + _dot(jnp.tanh(wd).astype(BF16), wup_ref[...]))) - 0.5
    logw = -jnp.exp(w_loglog)
    a = _sigmoid(a0_ref[...] + _dot(ad.astype(BF16), aup_ref[...]))

    if first:
        vfo_ref[...] = v
    else:
        low = _dot(v.astype(BF16), vdn_ref[...])
        nu = _sigmoid(v0_ref[...] + _dot(low.astype(BF16), vup_ref[...]))
        v = v + (vf_ref[...] - v) * nu

    seg = seg_ref[...]
    kk = k * kk_ref[...]
    kk = kk * lax.rsqrt(jnp.maximum(_seg_sum(kk * kk, seg), KK_EPS * KK_EPS))
    k = k * (1.0 + (a - 1.0) * ka_ref[...])
    b = kk * a

    cs = _dot_0_1(tri_ref[...], logw, 2)
    w_inv = jnp.exp(-cs)
    at_s[...] = (-kk * jnp.exp(cs - logw)).astype(BF16)
    rt_s[...] = (r * jnp.exp(cs)).astype(BF16)
    bt_s[...] = (b * w_inv).astype(BF16)
    kt_s[...] = (k * w_inv).astype(BF16)
    v_s[...] = v.astype(BF16)
    for i in range(n_chunks):
        wc_s[i:i + 1, :] = jnp.exp(cs[(i + 1) * c - 1:(i + 1) * c])

    bonus = _seg_sum(r * k * rk_ref[...], seg) * v
    g = z_ref[:, 3 * r_dim:4 * r_dim]
    gate = g * _sigmoid(g)

    lane = lax.broadcasted_iota(jnp.int32, (c, V7X_LANES), 1)
    t_row = lax.broadcasted_iota(jnp.int32, (c, V7X_LANES), 0)
    s_col = lane % head
    lo_half = lane < head
    hi_half = lane >= head
    strict_lo = lo_half & (s_col < t_row)
    strict_hi = hi_half & (s_col < t_row)
    incl = s_col <= t_row
    eye = (lax.broadcasted_iota(jnp.int32, (2 * c, 2 * c), 0)
           == lax.broadcasted_iota(jnp.int32, (2 * c, 2 * c), 1)).astype(F32)
    same_head = (lax.broadcasted_iota(jnp.int32, (V7X_LANES, V7X_LANES), 0) // head
                 == lax.broadcasted_iota(jnp.int32, (V7X_LANES, V7X_LANES), 1) // head)

    def tile(src, ci, p):
        return src[ci * c:(ci + 1) * c, p * V7X_LANES:(p + 1) * V7X_LANES]

    def head_rows(xv):
        zero = jnp.zeros_like(xv)
        return jnp.concatenate([jnp.where(lo_half, xv, zero), jnp.where(lo_half, zero, xv)], axis=0)

    problems = [(ci, p) for ci in range(n_chunks) for p in range(n_pairs)]
    lw = V7X_LANES

    for q, (ci, p) in enumerate(problems):
        bt, kt = tile(bt_s, ci, p), tile(kt_s, ci, p)
        zero = jnp.zeros_like(bt)
        rhs = jnp.concatenate([jnp.where(lo_half, bt, zero), jnp.where(lo_half, kt, zero),
                               jnp.where(lo_half, zero, kt), jnp.where(lo_half, zero, bt)], axis=0)
        lhs = jnp.concatenate([tile(at_s, ci, p), tile(rt_s, ci, p)], axis=0)
        gram = _dot_nt(lhs, rhs)
        g0, g1 = gram[0:c, 0:lw], gram[0:c, lw:2 * lw]
        nmat = jnp.concatenate([jnp.where(strict_lo, g0, 0.0), jnp.where(strict_hi, g1, 0.0)], axis=0)
        aak_s[q] = jnp.concatenate([jnp.where(strict_hi, g0, 0.0), jnp.where(strict_lo, g1, 0.0)],
                                   axis=0).astype(BF16)
        ar_s[q, 0] = jnp.where(incl, gram[c:2 * c, 0:lw], 0.0).astype(BF16)
        ar_s[q, 1] = jnp.where(incl, gram[c:2 * c, lw:2 * lw], 0.0).astype(BF16)
        tinv = eye + nmat
        tinv_s[q] = tinv
        ch_s[q, :, 0:lw] = nmat.astype(BF16)
        ch_s[q, :, lw:2 * lw] = tinv.astype(BF16)
    for q in range(len(problems)):
        pw = ch_s[q, :, 0:lw]
        ch_s[q, :, 0:lw] = _dot(pw, pw).astype(BF16)
    for _ in range(levels - 2):
        for q in range(len(problems)):
            res = _dot(ch_s[q, :, 0:lw], ch_s[q])
            tinv = tinv_s[q] + res[:, lw:2 * lw]
            tinv_s[q] = tinv
            ch_s[q, :, 0:lw] = res[:, 0:lw].astype(BF16)
            ch_s[q, :, lw:2 * lw] = tinv.astype(BF16)
    for q in range(len(problems)):
        tinv = tinv_s[q] + _dot(ch_s[q, :, 0:lw], ch_s[q, :, lw:2 * lw])
        ch_s[q, :, lw:2 * lw] = tinv.astype(BF16)
    for q, (ci, p) in enumerate(problems):
        vt = tile(v_s, ci, p)
        xloc_s[q] = _dot(aak_s[q], jnp.concatenate([vt, vt], axis=0)).astype(BF16)
    for q, (ci, p) in enumerate(problems):
        tu = _dot(ch_s[q, :, lw:2 * lw], jnp.concatenate([head_rows(tile(at_s, ci, p)), xloc_s[q]], axis=1))
        ta_s[q] = tu[:, 0:lw].astype(BF16)
        uloc_s[q] = tu[:, lw:2 * lw]

    for ci in range(n_chunks):
        for p in range(n_pairs):
            q = ci * n_pairs + p
            lhs = jnp.concatenate([ta_s[q], tile(rt_s, ci, p)], axis=0)
            xh = _dot_nt(lhs, state_ref[p].astype(BF16))
            u2 = xh[0:2 * c] + uloc_s[q]
            u_s[p] = jnp.where(lo_half, u2[0:c], u2[c:2 * c]).astype(BF16)
            yr_s[p] = xh[2 * c:3 * c]
        for p in range(n_pairs):
            q = ci * n_pairs + p
            ls = slice(p * lw, (p + 1) * lw)
            ut, vt = u_s[p], tile(v_s, ci, p)
            uv = jnp.concatenate([ut, vt], axis=0)
            upd = _dot_tn(uv, jnp.concatenate([tile(bt_s, ci, p), tile(kt_s, ci, p)], axis=0))
            state_ref[p] = (state_ref[p] + jnp.where(same_head, upd, 0.0)) * wc_s[ci:ci + 1, ls]
            y0 = _dot(ar_s[q, 0], uv)
            y1 = _dot(ar_s[q, 1], jnp.concatenate([vt, ut], axis=0))
            y_s[ci * c:(ci + 1) * c, ls] = yr_s[p] + jnp.where(lo_half, y0, y1)

    y = y_s[...]
    segn = segn_ref[...]
    yc = y - _seg_sum(y, segn)
    yn = yc * lax.rsqrt(_seg_sum(yc * yc, segn) + GN_EPS) * lnw_ref[...] + lnb_ref[...]
    y_ref[...] = (yn + bonus) * gate


def _rwkv_constants(tb, head):
    c = CHUNK
    i = np.arange(tb)
    tri = ((i[:, None] // c) == (i[None, :] // c)) & (i[None, :] <= i[:, None])
    lanes = np.arange(V7X_MXU_DIM) // head
    seg = (lanes[:, None] == lanes[None, :]).astype(np.float32)
    return jnp.asarray(seg, BF16), jnp.asarray(seg / head, BF16), jnp.asarray(tri, BF16)


def _rwkv(z, z_l, vfirst, prm, *, batch, seq, tb, r_dim, heads, head, first):
    m = batch * seq
    nt = seq // tb
    d_lora = prm["w_up"].shape[0]
    l_dim = d_lora + prm["a_up"].shape[0]
    n_pairs = heads // 2
    seg, segn, tri = _rwkv_constants(tb, head)
    n_prob = (tb // CHUNK) * n_pairs
    c2 = 2 * CHUNK

    def rows(width, col):
        return pl.BlockSpec((tb, width), lambda bi, ti: (bi * nt + ti, col))

    def whole(arr):
        return pl.BlockSpec(arr.shape, lambda bi, ti: (0,) * arr.ndim, pipeline_mode=pl.Buffered(1))

    ops = [z, z_l]
    specs = [rows(4 * r_dim, 0), rows(l_dim, 0)]
    if not first:
        ops.append(vfirst)
        specs.append(rows(r_dim, 0))
    names = ["mu_rkv", "mu_l", "w0", "w_up", "a0", "a_up"]
    if not first:
        names += ["v0", "v_down", "v_up"]
    names += ["k_k", "k_a", "r_k", "ln_w", "ln_b"]
    consts = [prm[nm] for nm in names] + [seg, segn, tri]
    ops += consts
    specs += [whole(a) for a in consts]

    out_shape = [jax.ShapeDtypeStruct((m, r_dim), F32)]
    out_specs = [rows(r_dim, 0)]
    if first:
        out_shape.append(jax.ShapeDtypeStruct((m, r_dim), F32))
        out_specs.append(rows(r_dim, 0))

    scratch = [
        pltpu.VMEM((8, 3 * r_dim), F32),
        pltpu.VMEM((8, l_dim), F32),
        pltpu.VMEM((n_pairs, V7X_LANES, V7X_LANES), F32),
    ] + [pltpu.VMEM((tb, r_dim), BF16) for _ in range(5)] + [
        pltpu.VMEM((8, r_dim), F32),
        pltpu.VMEM((tb, r_dim), F32),
        pltpu.VMEM((n_prob, c2, 2 * c2), BF16),
        pltpu.VMEM((n_prob, c2, c2), F32),
        pltpu.VMEM((n_prob, c2, c2), BF16),
        pltpu.VMEM((n_prob, 2, CHUNK, c2), BF16),
        pltpu.VMEM((n_prob, c2, V7X_LANES), BF16),
        pltpu.VMEM((n_prob, c2, V7X_LANES), BF16),
        pltpu.VMEM((n_prob, c2, V7X_LANES), F32),
        pltpu.VMEM((n_pairs, CHUNK, V7X_LANES), BF16),
        pltpu.VMEM((n_pairs, CHUNK, V7X_LANES), F32),
    ]
    res = pl.pallas_call(
        functools.partial(_rwkv_kernel, first=first, n_pairs=n_pairs, head=head, d_lora=d_lora),
        grid=(batch, nt),
        in_specs=specs,
        out_specs=out_specs,
        out_shape=out_shape,
        scratch_shapes=scratch,
        compiler_params=pltpu.CompilerParams(
            dimension_semantics=("parallel", "arbitrary"), vmem_limit_bytes=V7X_VMEM_LIMIT_BYTES),
        name="rwkv_first" if first else "rwkv",
    )(*ops)
    return (res[0], res[1]) if first else (res[0], vfirst)


def _pool_kernel(z_ref, wp_ref, ps_ref, y_ref, carry_ref):
    tb = z_ref.shape[0]
    p_dim = ps_ref.shape[1]
    cg = p_dim // len(POOL_WINDOWS)
    ti = pl.program_id(1)

    @pl.when(ti == 0)
    def _():
        carry_ref[...] = jnp.zeros_like(carry_ref)

    u = z_ref[:, 0:p_dim]
    gp = z_ref[:, p_dim:2 * p_dim]
    ext = jnp.concatenate([carry_ref[...], u], axis=0)
    carry_ref[...] = z_ref[tb - MAX_POOL_WINDOW:tb, 0:p_dim]

    sums = {}
    s = ext
    span = 1
    while span < MAX_POOL_WINDOW:
        s = s + pltpu.roll(s, span, 0)
        span *= 2
        sums[span] = s

    pos = (ti * tb + 1 + lax.broadcasted_iota(jnp.int32, (tb, 1), 0)).astype(F32)
    outs = []
    for gi, win in enumerate(POOL_WINDOWS):
        cols = slice(gi * cg, (gi + 1) * cg)
        mean = sums[win][MAX_POOL_WINDOW:, cols] / jnp.minimum(pos, float(win))
        d = mean - u[:, cols]
        outs.append(_dot(d.astype(BF16), wp_ref[gi]))
    y = jnp.concatenate(outs, axis=1)
    y_ref[...] = y * ps_ref[...] * (gp * _sigmoid(gp))


def _pool(z, w_pool, pool_scale, *, batch, seq, tb, r_dim, pool_dim):
    nt = seq // tb
    return pl.pallas_call(
        _pool_kernel,
        grid=(batch, nt),
        in_specs=[
            pl.BlockSpec((tb, 2 * pool_dim), lambda bi, ti: (bi * nt + ti, (4 * r_dim) // (2 * pool_dim))),
            pl.BlockSpec(w_pool.shape, lambda bi, ti: (0, 0, 0)),
            pl.BlockSpec(pool_scale.shape, lambda bi, ti: (0, 0)),
        ],
        out_specs=pl.BlockSpec((tb, pool_dim), lambda bi, ti: (bi * nt + ti, 0)),
        out_shape=jax.ShapeDtypeStruct((batch * seq, pool_dim), F32),
        scratch_shapes=[pltpu.VMEM((MAX_POOL_WINDOW, pool_dim), F32)],
        compiler_params=pltpu.CompilerParams(
            dimension_semantics=("parallel", "arbitrary"), vmem_limit_bytes=V7X_VMEM_LIMIT_BYTES),
        name="pool",
    )(z, w_pool, pool_scale)


def _out_kernel(x_ref, yr_ref, yp_ref, p_ref, wo_ref, wpg_ref, bpg_ref, wple_ref, fg_ref, o_ref, *, last):
    r_dim = yr_ref.shape[1]
    x1 = (x_ref[...]
          + _dot(yr_ref[...].astype(BF16), wo_ref[0:r_dim, :])
          + _dot(yp_ref[...].astype(BF16), wo_ref[r_dim:, :]))
    gate = _sigmoid(_dot(x1.astype(BF16), wpg_ref[...]) + bpg_ref[...])
    x2 = x1 + gate * _dot(p_ref[...].astype(BF16), wple_ref[...])
    if last:
        ms = jnp.mean(x2 * x2, axis=-1, keepdims=True)
        x2 = x2 * lax.rsqrt(ms + RMS_EPS) * fg_ref[...]
    o_ref[...] = x2


def _out(x2, yr, yp, p3, layer, w_out, w_pg, b_pg, w_ple, final_g, *, tm, last):
    m, d = x2.shape

    def whole(arr):
        return pl.BlockSpec(arr.shape, lambda i: (0,) * arr.ndim, pipeline_mode=pl.Buffered(1))

    return pl.pallas_call(
        functools.partial(_out_kernel, last=last),
        grid=(m // tm,),
        in_specs=[
            pl.BlockSpec((tm, d), lambda i: (i, 0)),
            pl.BlockSpec((tm, yr.shape[1]), lambda i: (i, 0)),
            pl.BlockSpec((tm, yp.shape[1]), lambda i: (i, 0)),
            pl.BlockSpec((None, tm, p3.shape[2]), lambda i: (layer, i, 0)),
            whole(w_out), whole(w_pg), whole(b_pg), whole(w_ple), whole(final_g),
        ],
        out_specs=pl.BlockSpec((tm, d), lambda i: (i, 0)),
        out_shape=jax.ShapeDtypeStruct((m, d), F32),
        compiler_params=pltpu.CompilerParams(
            dimension_semantics=("parallel",), vmem_limit_bytes=V7X_VMEM_LIMIT_BYTES),
        name="out_proj",
    )(x2, yr, yp, p3, w_out, w_pg, b_pg, w_ple, final_g)


def _tile(n, cap):
    t = min(n, cap)
    assert n % t == 0
    return t


def kernel(x, p, norm_g, w_in, mu, w0, w_up, a0, a_up, v0, v_down, v_up, k_k, k_a, r_k, ln_w, ln_b, w_pool,
           pool_scale, w_out, w_ple, w_pg, b_pg, final_g):
    batch, seq, d = x.shape
    depth = w_in.shape[0]
    r_dim = w0.shape[1]
    heads, head = r_k.shape[1], r_k.shape[2]
    pool_dim = pool_scale.shape[1]
    n_shift = mu.shape[1]
    l_dim = n_shift - 3 * r_dim
    m = batch * seq
    assert 2 * head == V7X_LANES and heads % 2 == 0 and heads * head == r_dim and r_dim % V7X_MXU_DIM == 0
    assert l_dim % V7X_LANES == 0 and (4 * r_dim + 2 * pool_dim) % l_dim == 0
    assert (4 * r_dim) % (2 * pool_dim) == 0 and pool_dim % len(POOL_WINDOWS) == 0

    tb_rwkv = _tile(seq, 256)
    tb_pool = _tile(seq, 512)
    tm_in = _tile(m, 512)
    tm_out = _tile(m, 256)
    assert tb_rwkv % CHUNK == 0 and tb_rwkv // CHUNK <= 8 and tb_pool >= MAX_POOL_WINDOW

    w_in_p = jnp.concatenate(
        [w_in[..., :3 * r_dim], w_in[..., n_shift:n_shift + r_dim], w_in[..., n_shift + r_dim:]],
        axis=-1).astype(BF16)
    w_in_l = w_in[..., 3 * r_dim:n_shift].astype(BF16)
    w_out_b = w_out.astype(BF16)
    w_pg_b = w_pg.astype(BF16)
    w_ple_b = w_ple.astype(BF16)
    w_pool_b = w_pool.astype(BF16)
    p3 = p.reshape(depth, m, p.shape[-1])
    x2 = x.reshape(m, d)
    fg = final_g.reshape(1, d)

    vfirst = None
    for i in range(depth):
        z, z_l = _in_proj(x2, norm_g[i].reshape(1, d), w_in_p[i], w_in_l[i], tm_in)
        prm = {
            "mu_rkv": mu[i, :3 * r_dim].reshape(1, -1), "mu_l": mu[i, 3 * r_dim:].reshape(1, -1),
            "w0": w0[i].reshape(1, -1), "w_up": w_up[i].astype(BF16),
            "a0": a0[i].reshape(1, -1), "a_up": a_up[i].astype(BF16),
            "k_k": k_k[i].reshape(1, -1), "k_a": k_a[i].reshape(1, -1), "r_k": r_k[i].reshape(1, -1),
            "ln_w": ln_w[i].reshape(1, -1), "ln_b": ln_b[i].reshape(1, -1),
        }
        if i > 0:
            prm.update({"v0": v0[i - 1].reshape(1, -1), "v_down": v_down[i - 1].astype(BF16),
                        "v_up": v_up[i - 1].astype(BF16)})
        yr, vfirst = _rwkv(z, z_l, vfirst, prm, batch=batch, seq=seq, tb=tb_rwkv, r_dim=r_dim,
                           heads=heads, head=head, first=(i == 0))
        yp = _pool(z, w_pool_b[i], pool_scale[i].reshape(1, -1), batch=batch, seq=seq, tb=tb_pool,
                   r_dim=r_dim, pool_dim=pool_dim)
        x2 = _out(x2, yr, yp, p3, i, w_out_b[i], w_pg_b[i], b_pg[i].reshape(1, d), w_ple_b[i], fg,
                  tm=tm_out, last=(i == depth - 1))
    return x2.reshape(batch, seq, d)
```

```python
import functools
import math

import jax
import jax.numpy as jnp
import numpy as np
from jax import lax
from jax.experimental import pallas as pl
from jax.experimental.pallas import tpu as pltpu

F32 = jnp.float32
BF16 = jnp.bfloat16

RMS_EPS = 1e-6
GN_EPS = 64e-5
KK_EPS = 1e-12
POOL_WINDOWS = (2, 4, 8, 16)

V7X_LANES = 128
V7X_MXU_DIM = 256
V7X_VMEM_LIMIT_BYTES = 56 * 2**20
CHUNK = 64
MAX_POOL_WINDOW = max(POOL_WINDOWS)


def _dot(a, b):
    return jnp.dot(a, b, preferred_element_type=F32)


def _dot_nt(a, b):
    return lax.dot_general(a, b, (((1,), (1,)), ((), ())), preferred_element_type=F32)


def _dot_tn(a, b):
    return lax.dot_general(a, b, (((0,), (0,)), ((), ())), preferred_element_type=F32)


def _sigmoid(x):
    return 1.0 / (1.0 + jnp.exp(-x))


def _softplus(x):
    return jnp.maximum(x, 0.0) + jnp.log(1.0 + jnp.exp(-jnp.abs(x)))


def _split_bf16(x, parts):
    out = []
    for _ in range(parts):
        hi = x.astype(BF16)
        out.append(hi)
        x = x - hi.astype(F32)
    return out


def _dot_0_1(m01, x, parts):
    acc = None
    for part in _split_bf16(x, parts):
        d = _dot(m01, part)
        acc = d if acc is None else acc + d
    return acc


def _seg_sum(x, seg):
    w = seg.shape[0]
    xb = x.astype(BF16)
    return jnp.concatenate(
        [_dot(xb[:, q * w:(q + 1) * w], seg) for q in range(x.shape[1] // w)], axis=1)


def _in_proj_kernel(x_ref, g_ref, w_ref, wl_ref, mu_ref, mul_ref, o_ref, ol_ref, h_scr, carry_ref, carryl_ref, *,
                    n_shift_tiles, gate_tiles, blocks_per_seq, row_chunk):
    i = pl.program_id(0)
    j = pl.program_id(1)
    tm = x_ref.shape[0]
    n_rc = tm // row_chunk
    keep = (i % blocks_per_seq != 0).astype(F32)
    row = lax.broadcasted_iota(jnp.int32, (row_chunk, 1), 0)

    def rows(rc):
        return slice(rc * row_chunk, (rc + 1) * row_chunk)

    def shift_mix(zv, prev_row, mu):
        prev = jnp.where(row == 0, prev_row, pltpu.roll(zv, 1, 0))
        return zv + (prev - zv) * mu

    def normalize(rc):
        x = x_ref[rows(rc), :]
        ms = jnp.mean(x * x, axis=-1, keepdims=True)
        h = (x * lax.rsqrt(ms + RMS_EPS) * g_ref[...]).astype(BF16)
        h_scr[rows(rc), :] = h
        return h

    def shifted_tile(with_norm):
        prev = carry_ref[j, 0:1, :] * keep
        prev_l = carryl_ref[0:1, :] * keep
        for rc in range(n_rc):
            h = normalize(rc) if with_norm else h_scr[rows(rc), :]
            z = _dot(h, w_ref[...])
            o_ref[rows(rc), :] = shift_mix(z, prev, mu_ref[...])
            prev = z[row_chunk - 1:row_chunk]
            if with_norm:
                zl = _dot(h, wl_ref[...])
                ol_ref[rows(rc), :] = shift_mix(zl, prev_l, mul_ref[...])
                prev_l = zl[row_chunk - 1:row_chunk]
        carry_ref[j, 0:1, :] = prev
        if with_norm:
            carryl_ref[0:1, :] = prev_l

    @pl.when((i == 0) & (j == 0))
    def _():
        carry_ref[...] = jnp.zeros_like(carry_ref)
        carryl_ref[...] = jnp.zeros_like(carryl_ref)

    is_gate = functools.reduce(lambda a, b: a | b, [j == t for t in gate_tiles])

    @pl.when(j == 0)
    def _():
        shifted_tile(True)

    @pl.when((j > 0) & (j < n_shift_tiles))
    def _():
        shifted_tile(False)

    @pl.when(is_gate)
    def _():
        for rc in range(n_rc):
            z = _dot(h_scr[rows(rc), :], w_ref[...])
            o_ref[rows(rc), :] = z * _sigmoid(z)

    @pl.when((j >= n_shift_tiles) & jnp.logical_not(is_gate))
    def _():
        for rc in range(n_rc):
            o_ref[rows(rc), :] = _dot(h_scr[rows(rc), :], w_ref[...])


def _in_proj(x2, g, w, w_l, mu_main, mu_l, *, tm, tn, seq, n_shift_tiles, gate_tiles):
    m, d = x2.shape
    n = w.shape[1]
    n_l = w_l.shape[1]
    n_tiles = n // tn
    return pl.pallas_call(
        functools.partial(_in_proj_kernel, n_shift_tiles=n_shift_tiles, gate_tiles=gate_tiles,
                          blocks_per_seq=seq // tm, row_chunk=min(tm, V7X_MXU_DIM)),
        grid=(m // tm, n_tiles),
        in_specs=[
            pl.BlockSpec((tm, d), lambda i, j: (i, 0)),
            pl.BlockSpec((1, d), lambda i, j: (0, 0)),
            pl.BlockSpec((d, tn), lambda i, j: (0, j)),
            pl.BlockSpec((d, n_l), lambda i, j: (0, 0)),
            pl.BlockSpec((1, tn), lambda i, j: (0, jnp.minimum(j, n_shift_tiles - 1))),
            pl.BlockSpec((1, n_l), lambda i, j: (0, 0)),
        ],
        out_specs=[pl.BlockSpec((tm, tn), lambda i, j: (i, j)),
                   pl.BlockSpec((tm, n_l), lambda i, j: (i, 0))],
        out_shape=[jax.ShapeDtypeStruct((m, n), F32), jax.ShapeDtypeStruct((m, n_l), F32)],
        scratch_shapes=[pltpu.VMEM((tm, d), BF16),
                        pltpu.VMEM((n_shift_tiles, 8, tn), F32),
                        pltpu.VMEM((8, n_l), F32)],
        compiler_params=pltpu.CompilerParams(
            dimension_semantics=("arbitrary", "arbitrary"), vmem_limit_bytes=V7X_VMEM_LIMIT_BYTES),
        name="in_proj",
    )(x2, g, w, w_l, mu_main, mu_l)


def _rwkv_kernel(*refs, first, n_pairs, head, d_lora):
    it = iter(refs)
    z_ref, zl_ref = next(it), next(it)
    vf_ref = None if first else next(it)
    w0_ref, wup_ref, a0_ref, aup_ref = (next(it) for _ in range(4))
    if not first:
        v0_ref, vdn_ref, vup_ref = next(it), next(it), next(it)
    kk_ref, ka_ref, rk_ref, lnw_ref, lnb_ref, seg_ref, segn_ref, tri_ref = (next(it) for _ in range(8))
    y_ref = next(it)
    vfo_ref = next(it) if first else None
    (state_ref, at_s, rt_s, bt_s, kt_s, v_s, wc_s, y_s,
     ch_s, tinv_s, aak_s, ar_s, xloc_s, ta_s, uloc_s, u_s, yr_s) = it

    tb = z_ref.shape[0]
    r_dim = w0_ref.shape[1]
    c = CHUNK
    n_chunks = tb // c
    levels = int(math.log2(c))

    @pl.when(pl.program_id(1) == 0)
    def _():
        state_ref[...] = jnp.zeros_like(state_ref)

    r = z_ref[:, 0:r_dim]
    k = z_ref[:, r_dim:2 * r_dim]
    v = z_ref[:, 2 * r_dim:3 * r_dim]
    lora_in = zl_ref[...]

    wd = lora_in[:, 0:d_lora]
    ad = lora_in[:, d_lora:]
    w_loglog = -_softplus(-(w0_ref[...] + _dot(jnp.tanh(wd).astype(BF16), wup_ref[...]))) - 0.5
    logw = -jnp.exp(w_loglog)
    a = _sigmoid(a0_ref[...] + _dot(ad.astype(BF16), aup_ref[...]))

    if first:
        vfo_ref[...] = v
    else:
        low = _dot(v.astype(BF16), vdn_ref[...])
        nu = _sigmoid(v0_ref[...] + _dot(low.astype(BF16), vup_ref[...]))
        v = v + (vf_ref[...] - v) * nu

    seg = seg_ref[...]
    kk = k * kk_ref[...]
    kk = kk * lax.rsqrt(jnp.maximum(_seg_sum(kk * kk, seg), KK_EPS * KK_EPS))
    k = k * (1.0 + (a - 1.0) * ka_ref[...])
    b = kk * a

    cs = _dot_0_1(tri_ref[...], logw, 2)
    w_inv = jnp.exp(-cs)
    at_s[...] = (-kk * jnp.exp(cs - logw)).astype(BF16)
    rt_s[...] = (r * jnp.exp(cs)).astype(BF16)
    bt_s[...] = (b * w_inv).astype(BF16)
    kt_s[...] = (k * w_inv).astype(BF16)
    v_s[...] = v.astype(BF16)
    for i in range(n_chunks):
        wc_s[i:i + 1, :] = jnp.exp(cs[(i + 1) * c - 1:(i + 1) * c])

    bonus = _seg_sum(r * k * rk_ref[...], seg) * v
    gate = z_ref[:, 3 * r_dim:4 * r_dim]

    lane = lax.broadcasted_iota(jnp.int32, (c, V7X_LANES), 1)
    t_row = lax.broadcasted_iota(jnp.int32, (c, V7X_LANES), 0)
    s_col = lane % head
    lo_half = lane < head
    hi_half = lane >= head
    strict_lo = lo_half & (s_col < t_row)
    strict_hi = hi_half & (s_col < t_row)
    incl = s_col <= t_row
    eye = (lax.broadcasted_iota(jnp.int32, (2 * c, 2 * c), 0)
           == lax.broadcasted_iota(jnp.int32, (2 * c, 2 * c), 1)).astype(F32)
    same_head = (lax.broadcasted_iota(jnp.int32, (V7X_LANES, V7X_LANES), 0) // head
                 == lax.broadcasted_iota(jnp.int32, (V7X_LANES, V7X_LANES), 1) // head)

    def tile(src, ci, p):
        return src[ci * c:(ci + 1) * c, p * V7X_LANES:(p + 1) * V7X_LANES]

    def head_rows(xv):
        zero = jnp.zeros_like(xv)
        return jnp.concatenate([jnp.where(lo_half, xv, zero), jnp.where(lo_half, zero, xv)], axis=0)

    problems = [(ci, p) for ci in range(n_chunks) for p in range(n_pairs)]
    lw = V7X_LANES

    for q, (ci, p) in enumerate(problems):
        bt, kt = tile(bt_s, ci, p), tile(kt_s, ci, p)
        zero = jnp.zeros_like(bt)
        rhs = jnp.concatenate([jnp.where(lo_half, bt, zero), jnp.where(lo_half, kt, zero),
                               jnp.where(lo_half, zero, kt), jnp.where(lo_half, zero, bt)], axis=0)
        lhs = jnp.concatenate([tile(at_s, ci, p), tile(rt_s, ci, p)], axis=0)
        gram = _dot_nt(lhs, rhs)
        g0, g1 = gram[0:c, 0:lw], gram[0:c, lw:2 * lw]
        nmat = jnp.concatenate([jnp.where(strict_lo, g0, 0.0), jnp.where(strict_hi, g1, 0.0)], axis=0)
        aak_s[q] = jnp.concatenate([jnp.where(strict_hi, g0, 0.0), jnp.where(strict_lo, g1, 0.0)],
                                   axis=0).astype(BF16)
        ar_s[q, 0] = jnp.where(incl, gram[c:2 * c, 0:lw], 0.0).astype(BF16)
        ar_s[q, 1] = jnp.where(incl, gram[c:2 * c, lw:2 * lw], 0.0).astype(BF16)
        tinv = eye + nmat
        tinv_s[q] = tinv
        ch_s[q, :, 0:lw] = nmat.astype(BF16)
        ch_s[q, :, lw:2 * lw] = tinv.astype(BF16)
    for q in range(len(problems)):
        pw = ch_s[q, :, 0:lw]
        ch_s[q, :, 0:lw] = _dot(pw, pw).astype(BF16)
    for _ in range(levels - 2):
        for q in range(len(problems)):
            res = _dot(ch_s[q, :, 0:lw], ch_s[q])
            tinv = tinv_s[q] + res[:, lw:2 * lw]
            tinv_s[q] = tinv
            ch_s[q, :, 0:lw] = res[:, 0:lw].astype(BF16)
            ch_s[q, :, lw:2 * lw] = tinv.astype(BF16)
    for q in range(len(problems)):
        tinv = tinv_s[q] + _dot(ch_s[q, :, 0:lw], ch_s[q, :, lw:2 * lw])
        ch_s[q, :, lw:2 * lw] = tinv.astype(BF16)
    for q, (ci, p) in enumerate(problems):
        vt = tile(v_s, ci, p)
        xloc_s[q] = _dot(aak_s[q], jnp.concatenate([vt, vt], axis=0)).astype(BF16)
    for q, (ci, p) in enumerate(problems):
        tu = _dot(ch_s[q, :, lw:2 * lw], jnp.concatenate([head_rows(tile(at_s, ci, p)), xloc_s[q]], axis=1))
        ta_s[q] = tu[:, 0:lw].astype(BF16)
        uloc_s[q] = tu[:, lw:2 * lw]

    for ci in range(n_chunks):
        for p in range(n_pairs):
            q = ci * n_pairs + p
            lhs = jnp.concatenate([ta_s[q], tile(rt_s, ci, p)], axis=0)
            xh = _dot_nt(lhs, state_ref[p].astype(BF16))
            u2 = xh[0:2 * c] + uloc_s[q]
            u_s[p] = jnp.where(lo_half, u2[0:c], u2[c:2 * c]).astype(BF16)
            yr_s[p] = xh[2 * c:3 * c]
        for p in range(n_pairs):
            q = ci * n_pairs + p
            ls = slice(p * lw, (p + 1) * lw)
            ut, vt = u_s[p], tile(v_s, ci, p)
            uv = jnp.concatenate([ut, vt], axis=0)
            upd = _dot_tn(uv, jnp.concatenate([tile(bt_s, ci, p), tile(kt_s, ci, p)], axis=0))
            state_ref[p] = (state_ref[p] + jnp.where(same_head, upd, 0.0)) * wc_s[ci:ci + 1, ls]
            y0 = _dot(ar_s[q, 0], uv)
            y1 = _dot(ar_s[q, 1], jnp.concatenate([vt, ut], axis=0))
            y_s[ci * c:(ci + 1) * c, ls] = yr_s[p] + jnp.where(lo_half, y0, y1)

    y = y_s[...]
    segn = segn_ref[...]
    yc = y - _seg_sum(y, segn)
    yn = yc * lax.rsqrt(_seg_sum(yc * yc, segn) + GN_EPS) * lnw_ref[...] + lnb_ref[...]
    y_ref[...] = (yn + bonus) * gate


def _rwkv_constants(tb, head):
    c = CHUNK
    i = np.arange(tb)
    tri = ((i[:, None] // c) == (i[None, :] // c)) & (i[None, :] <= i[:, None])
    lanes = np.arange(V7X_MXU_DIM) // head
    seg = (lanes[:, None] == lanes[None, :]).astype(np.float32)
    return jnp.asarray(seg, BF16), jnp.asarray(seg / head, BF16), jnp.asarray(tri, BF16)


def _rwkv(z, z_l, vfirst, prm, *, batch, seq, tb, r_dim, heads, head, first):
    m = batch * seq
    nt = seq // tb
    d_lora = prm["w_up"].shape[0]
    l_dim = d_lora + prm["a_up"].shape[0]
    n_pairs = heads // 2
    seg, segn, tri = _rwkv_constants(tb, head)
    n_prob = (tb // CHUNK) * n_pairs
    c2 = 2 * CHUNK

    def rows(width, col):
        return pl.BlockSpec((tb, width), lambda bi, ti: (bi * nt + ti, col))

    def whole(arr):
        return pl.BlockSpec(arr.shape, lambda bi, ti: (0,) * arr.ndim, pipeline_mode=pl.Buffered(1))

    ops = [z, z_l]
    specs = [rows(4 * r_dim, 0), rows(l_dim, 0)]
    if not first:
        ops.append(vfirst)
        specs.append(rows(r_dim, 0))
    names = ["w0", "w_up", "a0", "a_up"]
    if not first:
        names += ["v0", "v_down", "v_up"]
    names += ["k_k", "k_a", "r_k", "ln_w", "ln_b"]
    consts = [prm[nm] for nm in names] + [seg, segn, tri]
    ops += consts
    specs += [whole(a) for a in consts]

    out_shape = [jax.ShapeDtypeStruct((m, r_dim), F32)]
    out_specs = [rows(r_dim, 0)]
    if first:
        out_shape.append(jax.ShapeDtypeStruct((m, r_dim), F32))
        out_specs.append(rows(r_dim, 0))

    scratch = [
        pltpu.VMEM((n_pairs, V7X_LANES, V7X_LANES), F32),
    ] + [pltpu.VMEM((tb, r_dim), BF16) for _ in range(5)] + [
        pltpu.VMEM((8, r_dim), F32),
        pltpu.VMEM((tb, r_dim), F32),
        pltpu.VMEM((n_prob, c2, 2 * c2), BF16),
        pltpu.VMEM((n_prob, c2, c2), F32),
        pltpu.VMEM((n_prob, c2, c2), BF16),
        pltpu.VMEM((n_prob, 2, CHUNK, c2), BF16),
        pltpu.VMEM((n_prob, c2, V7X_LANES), BF16),
        pltpu.VMEM((n_prob, c2, V7X_LANES), BF16),
        pltpu.VMEM((n_prob, c2, V7X_LANES), F32),
        pltpu.VMEM((n_pairs, CHUNK, V7X_LANES), BF16),
        pltpu.VMEM((n_pairs, CHUNK, V7X_LANES), F32),
    ]
    res = pl.pallas_call(
        functools.partial(_rwkv_kernel, first=first, n_pairs=n_pairs, head=head, d_lora=d_lora),
        grid=(batch, nt),
        in_specs=specs,
        out_specs=out_specs,
        out_shape=out_shape,
        scratch_shapes=scratch,
        compiler_params=pltpu.CompilerParams(
            dimension_semantics=("parallel", "arbitrary"), vmem_limit_bytes=V7X_VMEM_LIMIT_BYTES),
        name="rwkv_first" if first else "rwkv",
    )(*ops)
    return (res[0], res[1]) if first else (res[0], vfirst)


def _pool_kernel(z_ref, wp_ref, ps_ref, y_ref, carry_ref):
    tb = z_ref.shape[0]
    p_dim = ps_ref.shape[1]
    cg = p_dim // len(POOL_WINDOWS)
    ti = pl.program_id(1)

    @pl.when(ti == 0)
    def _():
        carry_ref[...] = jnp.zeros_like(carry_ref)

    u = z_ref[:, 0:p_dim]
    gp = z_ref[:, p_dim:2 * p_dim]
    ext = jnp.concatenate([carry_ref[...], u], axis=0)
    carry_ref[...] = z_ref[tb - MAX_POOL_WINDOW:tb, 0:p_dim]

    sums = {}
    s = ext
    span = 1
    while span < MAX_POOL_WINDOW:
        s = s + pltpu.roll(s, span, 0)
        span *= 2
        sums[span] = s

    pos = (ti * tb + 1 + lax.broadcasted_iota(jnp.int32, (tb, 1), 0)).astype(F32)
    outs = []
    for gi, win in enumerate(POOL_WINDOWS):
        cols = slice(gi * cg, (gi + 1) * cg)
        mean = sums[win][MAX_POOL_WINDOW:, cols] / jnp.minimum(pos, float(win))
        d = mean - u[:, cols]
        outs.append(_dot(d.astype(BF16), wp_ref[gi]))
    y = jnp.concatenate(outs, axis=1)
    y_ref[...] = y * ps_ref[...] * gp


def _pool(z, w_pool, pool_scale, *, batch, seq, tb, r_dim, pool_dim):
    nt = seq // tb
    return pl.pallas_call(
        _pool_kernel,
        grid=(batch, nt),
        in_specs=[
            pl.BlockSpec((tb, 2 * pool_dim), lambda bi, ti: (bi * nt + ti, (4 * r_dim) // (2 * pool_dim))),
            pl.BlockSpec(w_pool.shape, lambda bi, ti: (0, 0, 0)),
            pl.BlockSpec(pool_scale.shape, lambda bi, ti: (0, 0)),
        ],
        out_specs=pl.BlockSpec((tb, pool_dim), lambda bi, ti: (bi * nt + ti, 0)),
        out_shape=jax.ShapeDtypeStruct((batch * seq, pool_dim), F32),
        scratch_shapes=[pltpu.VMEM((MAX_POOL_WINDOW, pool_dim), F32)],
        compiler_params=pltpu.CompilerParams(
            dimension_semantics=("parallel", "arbitrary"), vmem_limit_bytes=V7X_VMEM_LIMIT_BYTES),
        name="pool",
    )(z, w_pool, pool_scale)


def _out_kernel(x_ref, yr_ref, yp_ref, p_ref, wo_ref, wpg_ref, bpg_ref, wple_ref, fg_ref, o_ref, *, last):
    r_dim = yr_ref.shape[1]
    x1 = (x_ref[...]
          + _dot(yr_ref[...].astype(BF16), wo_ref[0:r_dim, :])
          + _dot(yp_ref[...].astype(BF16), wo_ref[r_dim:, :]))
    gate = _sigmoid(_dot(x1.astype(BF16), wpg_ref[...]) + bpg_ref[...])
    x2 = x1 + gate * _dot(p_ref[...].astype(BF16), wple_ref[...])
    if last:
        ms = jnp.mean(x2 * x2, axis=-1, keepdims=True)
        x2 = x2 * lax.rsqrt(ms + RMS_EPS) * fg_ref[...]
    o_ref[...] = x2


def _out(x2, yr, yp, p3, layer, w_out, w_pg, b_pg, w_ple, final_g, *, tm, last):
    m, d = x2.shape

    def whole(arr):
        return pl.BlockSpec(arr.shape, lambda i: (0,) * arr.ndim, pipeline_mode=pl.Buffered(1))

    return pl.pallas_call(
        functools.partial(_out_kernel, last=last),
        grid=(m // tm,),
        in_specs=[
            pl.BlockSpec((tm, d), lambda i: (i, 0)),
            pl.BlockSpec((tm, yr.shape[1]), lambda i: (i, 0)),
            pl.BlockSpec((tm, yp.shape[1]), lambda i: (i, 0)),
            pl.BlockSpec((None, tm, p3.shape[2]), lambda i: (layer, i, 0)),
            whole(w_out), whole(w_pg), whole(b_pg), whole(w_ple), whole(final_g),
        ],
        out_specs=pl.BlockSpec((tm, d), lambda i: (i, 0)),
        out_shape=jax.ShapeDtypeStruct((m, d), F32),
        compiler_params=pltpu.CompilerParams(
            dimension_semantics=("parallel",), vmem_limit_bytes=V7X_VMEM_LIMIT_BYTES),
        name="out_proj",
    )(x2, yr, yp, p3, w_out, w_pg, b_pg, w_ple, final_g)


def _tile(n, cap):
    t = min(n, cap)
    assert n % t == 0
    return t


def kernel(x, p, norm_g, w_in, mu, w0, w_up, a0, a_up, v0, v_down, v_up, k_k, k_a, r_k, ln_w, ln_b, w_pool,
           pool_scale, w_out, w_ple, w_pg, b_pg, final_g):
    batch, seq, d = x.shape
    depth = w_in.shape[0]
    r_dim = w0.shape[1]
    heads, head = r_k.shape[1], r_k.shape[2]
    pool_dim = pool_scale.shape[1]
    n_shift = mu.shape[1]
    l_dim = n_shift - 3 * r_dim
    m = batch * seq
    assert 2 * head == V7X_LANES and heads % 2 == 0 and heads * head == r_dim and r_dim % V7X_MXU_DIM == 0
    assert l_dim % V7X_LANES == 0
    assert pool_dim == r_dim and pool_dim % len(POOL_WINDOWS) == 0

    tb_rwkv = _tile(seq, 256)
    tb_pool = _tile(seq, 512)
    tm_in = _tile(seq, 1024)
    tm_out = _tile(m, 256)
    assert tb_rwkv % CHUNK == 0 and tb_rwkv // CHUNK <= 8 and tb_pool >= MAX_POOL_WINDOW

    w_in_p = jnp.concatenate(
        [w_in[..., :3 * r_dim], w_in[..., n_shift:n_shift + r_dim], w_in[..., n_shift + r_dim:]],
        axis=-1).astype(BF16)
    w_in_l = w_in[..., 3 * r_dim:n_shift].astype(BF16)
    w_out_b = w_out.astype(BF16)
    w_pg_b = w_pg.astype(BF16)
    w_ple_b = w_ple.astype(BF16)
    w_pool_b = w_pool.astype(BF16)
    p3 = p.reshape(depth, m, p.shape[-1])
    x2 = x.reshape(m, d)
    fg = final_g.reshape(1, d)

    vfirst = None
    for i in range(depth):
        z, z_l = _in_proj(x2, norm_g[i].reshape(1, d), w_in_p[i], w_in_l[i], mu[i, :3 * r_dim].reshape(1, -1),
                          mu[i, 3 * r_dim:].reshape(1, -1), tm=tm_in, tn=r_dim, seq=seq, n_shift_tiles=3,
                          gate_tiles=(3, 5))
        prm = {
            "w0": w0[i].reshape(1, -1), "w_up": w_up[i].astype(BF16),
            "a0": a0[i].reshape(1, -1), "a_up": a_up[i].astype(BF16),
            "k_k": k_k[i].reshape(1, -1), "k_a": k_a[i].reshape(1, -1), "r_k": r_k[i].reshape(1, -1),
            "ln_w": ln_w[i].reshape(1, -1), "ln_b": ln_b[i].reshape(1, -1),
        }
        if i > 0:
            prm.update({"v0": v0[i - 1].reshape(1, -1), "v_down": v_down[i - 1].astype(BF16),
                        "v_up": v_up[i - 1].astype(BF16)})
        yr, vfirst = _rwkv(z, z_l, vfirst, prm, batch=batch, seq=seq, tb=tb_rwkv, r_dim=r_dim,
                           heads=heads, head=head, first=(i == 0))
        yp = _pool(z, w_pool_b[i], pool_scale[i].reshape(1, -1), batch=batch, seq=seq, tb=tb_pool,
                   r_dim=r_dim, pool_dim=pool_dim)
        x2 = _out(x2, yr, yp, p3, i, w_out_b[i], w_pg_b[i], b_pg[i].reshape(1, d), w_ple_b[i], fg,
                  tm=tm_out, last=(i == depth - 1))
    return x2.reshape(batch, seq, d)
```

```python
import functools
import math

import jax
import jax.numpy as jnp
import numpy as np
from jax import lax
from jax.experimental import pallas as pl
from jax.experimental.pallas import tpu as pltpu

F32 = jnp.float32
BF16 = jnp.bfloat16

RMS_EPS = 1e-6
GN_EPS = 64e-5
KK_EPS = 1e-12
POOL_WINDOWS = (2, 4, 8, 16)

V7X_LANES = 128
V7X_MXU_DIM = 256
V7X_VMEM_LIMIT_BYTES = 56 * 2**20
CHUNK = 64
MAX_POOL_WINDOW = max(POOL_WINDOWS)


def _dot(a, b):
    return jnp.dot(a, b, preferred_element_type=F32)


def _dot_nt(a, b):
    return lax.dot_general(a, b, (((1,), (1,)), ((), ())), preferred_element_type=F32)


def _dot_tn(a, b):
    return lax.dot_general(a, b, (((0,), (0,)), ((), ())), preferred_element_type=F32)


def _sigmoid(x):
    return 1.0 / (1.0 + jnp.exp(-x))


def _softplus(x):
    return jnp.maximum(x, 0.0) + jnp.log(1.0 + jnp.exp(-jnp.abs(x)))


def _split_bf16(x, parts):
    out = []
    for _ in range(parts):
        hi = x.astype(BF16)
        out.append(hi)
        x = x - hi.astype(F32)
    return out


def _dot_0_1(m01, x, parts):
    acc = None
    for part in _split_bf16(x, parts):
        d = _dot(m01, part)
        acc = d if acc is None else acc + d
    return acc


def _seg_sum(x, seg):
    w = seg.shape[0]
    xb = x.astype(BF16)
    return jnp.concatenate(
        [_dot(xb[:, q * w:(q + 1) * w], seg) for q in range(x.shape[1] // w)], axis=1)


def _in_proj_kernel(x_ref, g_ref, w_ref, wl_ref, mu_ref, mul_ref, o_ref, ol_ref, h_scr, carry_ref, carryl_ref, *,
                    n_shift_tiles, gate_tiles, blocks_per_seq, row_chunk):
    i = pl.program_id(0)
    j = pl.program_id(1)
    tm = x_ref.shape[0]
    n_rc = tm // row_chunk
    keep = (i % blocks_per_seq != 0).astype(F32)
    row = lax.broadcasted_iota(jnp.int32, (row_chunk, 1), 0)

    def rows(rc):
        return slice(rc * row_chunk, (rc + 1) * row_chunk)

    def shift_mix(zv, prev_row, mu):
        prev = jnp.where(row == 0, prev_row, pltpu.roll(zv, 1, 0))
        return zv + (prev - zv) * mu

    def normalize(rc):
        x = x_ref[rows(rc), :]
        ms = jnp.mean(x * x, axis=-1, keepdims=True)
        h = (x * lax.rsqrt(ms + RMS_EPS) * g_ref[...]).astype(BF16)
        h_scr[rows(rc), :] = h
        return h

    def shifted_tile(with_norm):
        prev = carry_ref[j, 0:1, :] * keep
        prev_l = carryl_ref[0:1, :] * keep
        for rc in range(n_rc):
            h = normalize(rc) if with_norm else h_scr[rows(rc), :]
            z = _dot(h, w_ref[...])
            o_ref[rows(rc), :] = shift_mix(z, prev, mu_ref[...])
            prev = z[row_chunk - 1:row_chunk]
            if with_norm:
                zl = _dot(h, wl_ref[...])
                ol_ref[rows(rc), :] = shift_mix(zl, prev_l, mul_ref[...])
                prev_l = zl[row_chunk - 1:row_chunk]
        carry_ref[j, 0:1, :] = prev
        if with_norm:
            carryl_ref[0:1, :] = prev_l

    @pl.when((i == 0) & (j == 0))
    def _():
        carry_ref[...] = jnp.zeros_like(carry_ref)
        carryl_ref[...] = jnp.zeros_like(carryl_ref)

    is_gate = functools.reduce(lambda a, b: a | b, [j == t for t in gate_tiles])

    @pl.when(j == 0)
    def _():
        shifted_tile(True)

    @pl.when((j > 0) & (j < n_shift_tiles))
    def _():
        shifted_tile(False)

    @pl.when(is_gate)
    def _():
        for rc in range(n_rc):
            z = _dot(h_scr[rows(rc), :], w_ref[...])
            o_ref[rows(rc), :] = z * _sigmoid(z)

    @pl.when((j >= n_shift_tiles) & jnp.logical_not(is_gate))
    def _():
        for rc in range(n_rc):
            o_ref[rows(rc), :] = _dot(h_scr[rows(rc), :], w_ref[...])


def _in_proj(x2, g, w, w_l, mu_main, mu_l, *, tm, tn, seq, n_shift_tiles, gate_tiles):
    m, d = x2.shape
    n = w.shape[1]
    n_l = w_l.shape[1]
    n_tiles = n // tn
    return pl.pallas_call(
        functools.partial(_in_proj_kernel, n_shift_tiles=n_shift_tiles, gate_tiles=gate_tiles,
                          blocks_per_seq=seq // tm, row_chunk=min(tm, V7X_MXU_DIM)),
        grid=(m // tm, n_tiles),
        in_specs=[
            pl.BlockSpec((tm, d), lambda i, j: (i, 0)),
            pl.BlockSpec((1, d), lambda i, j: (0, 0)),
            pl.BlockSpec((d, tn), lambda i, j: (0, j)),
            pl.BlockSpec((d, n_l), lambda i, j: (0, 0)),
            pl.BlockSpec((1, tn), lambda i, j: (0, jnp.minimum(j, n_shift_tiles - 1))),
            pl.BlockSpec((1, n_l), lambda i, j: (0, 0)),
        ],
        out_specs=[pl.BlockSpec((tm, tn), lambda i, j: (i, j)),
                   pl.BlockSpec((tm, n_l), lambda i, j: (i, 0))],
        out_shape=[jax.ShapeDtypeStruct((m, n), F32), jax.ShapeDtypeStruct((m, n_l), F32)],
        scratch_shapes=[pltpu.VMEM((tm, d), BF16),
                        pltpu.VMEM((n_shift_tiles, 8, tn), F32),
                        pltpu.VMEM((8, n_l), F32)],
        compiler_params=pltpu.CompilerParams(
            dimension_semantics=("arbitrary", "arbitrary"), vmem_limit_bytes=V7X_VMEM_LIMIT_BYTES),
        name="in_proj",
    )(x2, g, w, w_l, mu_main, mu_l)


_PRE_AT, _PRE_RT, _PRE_BT, _PRE_KT, _PRE_V = range(5)
_PRE_BONUS, _PRE_GATE = range(2)
_WK_LOGW, _WK_B, _WK_KK, _WK_V, _WK_K2, _WK_CS, _WK_BSUM = range(7)
_WB_HI, _WB_LO, _WB_RKR = range(3)


def _interleave(main, filler):
    out = []
    done = 0
    for i, task in enumerate(main):
        want = (i + 1) * len(filler) // len(main)
        if want > done and _interleave.next_row is not None:
            row = _interleave.next_row
            _interleave.next_row += 1
            out.append(functools.partial(task, tok_row=row))
            out.extend(functools.partial(f, tok_row=row) for f in filler[done:want])
        else:
            out.append(task)
            out.extend(filler[done:want])
        done = want
    return out


def _rwkv_kernel(*refs, first, n_pairs, head, d_lora, nt):
    it = iter(refs)
    z_ref, zl_ref = next(it), next(it)
    vf_ref = None if first else next(it)
    w0_ref, wup_ref, a0_ref, aup_ref = (next(it) for _ in range(4))
    if not first:
        v0_ref, vdn_ref, vup_ref = next(it), next(it), next(it)
    kk_ref, ka_ref, rk_ref, lnw_ref, lnb_ref, seg_ref, segn_ref, tri_ref = (next(it) for _ in range(8))
    y_ref = next(it)
    vfo_ref = next(it) if first else None
    (state_ref, pre_b, pre_f, wc_s, wk_f, wk_b, y_s,
     ch_s, tinv_s, aak_s, ar_s, xloc_s, ta_s, uloc_s, u_s, yr_s, tok_s) = it

    tb = z_ref.shape[0]
    r_dim = w0_ref.shape[1]
    c = CHUNK
    n_chunks = tb // c
    levels = int(math.log2(c))
    lw = V7X_LANES
    gw = V7X_MXU_DIM
    n_groups = r_dim // gw
    step = pl.program_id(0)

    @pl.when(step == 0)
    def _():
        for ref in (state_ref, pre_b, pre_f, wc_s):
            ref[...] = jnp.zeros_like(ref)

    keep_state = ((step + nt - 1) % nt != 0).astype(F32)

    lane = lax.broadcasted_iota(jnp.int32, (c, lw), 1)
    t_row = lax.broadcasted_iota(jnp.int32, (c, lw), 0)
    s_col = lane % head
    lo_half = lane < head
    hi_half = lane >= head
    strict_lo = lo_half & (s_col < t_row)
    strict_hi = hi_half & (s_col < t_row)
    incl = s_col <= t_row
    eye = (lax.broadcasted_iota(jnp.int32, (2 * c, 2 * c), 0)
           == lax.broadcasted_iota(jnp.int32, (2 * c, 2 * c), 1)).astype(F32)
    same_head = (lax.broadcasted_iota(jnp.int32, (lw, lw), 0) // head
                 == lax.broadcasted_iota(jnp.int32, (lw, lw), 1) // head)

    def run(wr, rd):
        def post(tok_row, value):
            if tok_row is not None:
                tok_s[tok_row:tok_row + 1, :] = value[0:1, 0:lw]

        def token_zero(tok_row):
            z0 = tok_s[tok_row:tok_row + 1, :] * 0.0
            return jnp.concatenate([z0] * (gw // lw), axis=1)

        def prep_matmuls_1():
            zl = zl_ref[...]
            wk_f[_WK_LOGW] = _dot(jnp.tanh(zl[:, 0:d_lora]).astype(BF16), wup_ref[...])
            wk_f[_WK_B] = _dot(zl[:, d_lora:].astype(BF16), aup_ref[...])
            kk = z_ref[:, r_dim:2 * r_dim] * kk_ref[...]
            wk_f[_WK_KK] = _seg_sum(kk * kk, seg_ref[...])
            v = z_ref[:, 2 * r_dim:3 * r_dim]
            if first:
                vfo_ref[...] = v
            else:
                low = _dot(v.astype(BF16), vdn_ref[...])
                wk_f[_WK_V] = _dot(low.astype(BF16), vup_ref[...])

        def prep_tile_1(ci, g, tok_row):
            rows = slice(ci * c, (ci + 1) * c)
            cols = slice(g * gw, (g + 1) * gw)
            hold = token_zero(tok_row)
            r = z_ref[rows, cols]
            k = z_ref[rows, r_dim + g * gw:r_dim + (g + 1) * gw]
            v = z_ref[rows, 2 * r_dim + g * gw:2 * r_dim + (g + 1) * gw]
            w_loglog = -_softplus(-(w0_ref[:, cols] + hold + wk_f[_WK_LOGW, rows, cols])) - 0.5
            logw = -jnp.exp(w_loglog)
            a = _sigmoid(a0_ref[:, cols] + hold + wk_f[_WK_B, rows, cols])
            if not first:
                nu = _sigmoid(v0_ref[:, cols] + hold + wk_f[_WK_V, rows, cols])
                v = v + (vf_ref[rows, cols] - v) * nu
            kk = k * (kk_ref[:, cols] + hold)
            kk = kk * lax.rsqrt(jnp.maximum(wk_f[_WK_KK, rows, cols], KK_EPS * KK_EPS))
            k2 = k * (1.0 + (a - 1.0) * ka_ref[:, cols])
            wk_f[_WK_LOGW, rows, cols] = logw
            wk_f[_WK_B, rows, cols] = kk * a
            wk_f[_WK_KK, rows, cols] = kk
            wk_f[_WK_V, rows, cols] = v
            wk_f[_WK_K2, rows, cols] = k2
            hi = logw.astype(BF16)
            wk_b[_WB_HI, rows, cols] = hi
            wk_b[_WB_LO, rows, cols] = (logw - hi.astype(F32)).astype(BF16)
            wk_b[_WB_RKR, rows, cols] = (r * k2 * rk_ref[:, cols]).astype(BF16)
            pre_b[wr, _PRE_V, rows, cols] = v.astype(BF16)
            pre_f[wr, _PRE_GATE, rows, cols] = z_ref[rows, 3 * r_dim + g * gw:3 * r_dim + (g + 1) * gw]

        def prep_matmuls_2():
            tri = tri_ref[...]
            wk_f[_WK_CS] = _dot(tri, wk_b[_WB_HI]) + _dot(tri, wk_b[_WB_LO])
            seg = seg_ref[...]
            wk_f[_WK_BSUM] = jnp.concatenate(
                [_dot(wk_b[_WB_RKR, :, g * gw:(g + 1) * gw], seg) for g in range(n_groups)], axis=1)

        def prep_tile_2(ci, g, tok_row):
            rows = slice(ci * c, (ci + 1) * c)
            cols = slice(g * gw, (g + 1) * gw)
            cs = wk_f[_WK_CS, rows, cols] + token_zero(tok_row)
            w_inv = jnp.exp(-cs)
            pre_b[wr, _PRE_AT, rows, cols] = (
                -wk_f[_WK_KK, rows, cols] * jnp.exp(cs - wk_f[_WK_LOGW, rows, cols])).astype(BF16)
            pre_b[wr, _PRE_RT, rows, cols] = (z_ref[rows, cols] * jnp.exp(cs)).astype(BF16)
            pre_b[wr, _PRE_BT, rows, cols] = (wk_f[_WK_B, rows, cols] * w_inv).astype(BF16)
            pre_b[wr, _PRE_KT, rows, cols] = (wk_f[_WK_K2, rows, cols] * w_inv).astype(BF16)
            wc_s[wr, ci:ci + 1, cols] = jnp.exp(cs[c - 1:c])
            pre_f[wr, _PRE_BONUS, rows, cols] = wk_f[_WK_BSUM, rows, cols] * wk_f[_WK_V, rows, cols]

        tiles = [(ci, g) for ci in range(n_chunks) for g in range(n_groups)]
        prep_1 = [functools.partial(prep_tile_1, ci, g) for ci, g in tiles]
        prep_2 = [functools.partial(prep_tile_2, ci, g) for ci, g in tiles]

        def tile(plane, ci, p):
            return pre_b[rd, plane, ci * c:(ci + 1) * c, p * lw:(p + 1) * lw]

        def head_rows(xv):
            zero = jnp.zeros_like(xv)
            return jnp.concatenate([jnp.where(lo_half, xv, zero), jnp.where(lo_half, zero, xv)], axis=0)

        problems = [(ci, p) for ci in range(n_chunks) for p in range(n_pairs)]

        def gram(q, ci, p, tok_row=None):
            bt, kt = tile(_PRE_BT, ci, p), tile(_PRE_KT, ci, p)
            zero = jnp.zeros_like(bt)
            rhs = jnp.concatenate([jnp.where(lo_half, bt, zero), jnp.where(lo_half, kt, zero),
                                   jnp.where(lo_half, zero, kt), jnp.where(lo_half, zero, bt)], axis=0)
            lhs = jnp.concatenate([tile(_PRE_AT, ci, p), tile(_PRE_RT, ci, p)], axis=0)
            gm = _dot_nt(lhs, rhs)
            g0, g1 = gm[0:c, 0:lw], gm[0:c, lw:2 * lw]
            nmat = jnp.concatenate([jnp.where(strict_lo, g0, 0.0), jnp.where(strict_hi, g1, 0.0)], axis=0)
            aak_s[q] = jnp.concatenate([jnp.where(strict_hi, g0, 0.0), jnp.where(strict_lo, g1, 0.0)],
                                       axis=0).astype(BF16)
            ar_s[q, 0] = jnp.where(incl, gm[c:2 * c, 0:lw], 0.0).astype(BF16)
            ar_s[q, 1] = jnp.where(incl, gm[c:2 * c, lw:2 * lw], 0.0).astype(BF16)
            tinv = eye + nmat
            tinv_s[q] = tinv
            ch_s[q, :, 0:lw] = nmat.astype(BF16)
            ch_s[q, :, lw:2 * lw] = tinv.astype(BF16)
            post(tok_row, tinv)

        def square(q, tok_row=None):
            pw = ch_s[q, :, 0:lw]
            sq = _dot(pw, pw)
            ch_s[q, :, 0:lw] = sq.astype(BF16)
            post(tok_row, sq)

        def square_and_extend(q, tok_row=None):
            res = _dot(ch_s[q, :, 0:lw], ch_s[q])
            tinv = tinv_s[q] + res[:, lw:2 * lw]
            tinv_s[q] = tinv
            ch_s[q, :, 0:lw] = res[:, 0:lw].astype(BF16)
            ch_s[q, :, lw:2 * lw] = tinv.astype(BF16)
            post(tok_row, tinv)

        def extend_last(q, tok_row=None):
            tinv = tinv_s[q] + _dot(ch_s[q, :, 0:lw], ch_s[q, :, lw:2 * lw])
            ch_s[q, :, lw:2 * lw] = tinv.astype(BF16)
            post(tok_row, tinv)

        def local_x(q, ci, p, tok_row=None):
            vt = tile(_PRE_V, ci, p)
            xl = _dot(aak_s[q], jnp.concatenate([vt, vt], axis=0))
            xloc_s[q] = xl.astype(BF16)
            post(tok_row, xl)

        def apply_inverse(q, ci, p, tok_row=None):
            tu = _dot(ch_s[q, :, lw:2 * lw],
                      jnp.concatenate([head_rows(tile(_PRE_AT, ci, p)), xloc_s[q]], axis=1))
            ta_s[q] = tu[:, 0:lw].astype(BF16)
            uloc_s[q] = tu[:, lw:2 * lw]
            post(tok_row, tu)

        def read_state(ci, p, tok_row=None):
            q = ci * n_pairs + p
            s0 = state_ref[p] * keep_state if ci == 0 else state_ref[p]
            lhs = jnp.concatenate([ta_s[q], tile(_PRE_RT, ci, p)], axis=0)
            xh = _dot_nt(lhs, s0.astype(BF16))
            u2 = xh[0:2 * c] + uloc_s[q]
            u_s[p] = jnp.where(lo_half, u2[0:c], u2[c:2 * c]).astype(BF16)
            yr_s[p] = xh[2 * c:3 * c]
            post(tok_row, xh)

        def write_state(ci, p, tok_row=None):
            q = ci * n_pairs + p
            ls = slice(p * lw, (p + 1) * lw)
            ut, vt = u_s[p], tile(_PRE_V, ci, p)
            uv = jnp.concatenate([ut, vt], axis=0)
            upd = _dot_tn(uv, jnp.concatenate([tile(_PRE_BT, ci, p), tile(_PRE_KT, ci, p)], axis=0))
            s0 = state_ref[p] * keep_state if ci == 0 else state_ref[p]
            state_ref[p] = (s0 + jnp.where(same_head, upd, 0.0)) * wc_s[rd, ci:ci + 1, ls]
            y0 = _dot(ar_s[q, 0], uv)
            y1 = _dot(ar_s[q, 1], jnp.concatenate([vt, ut], axis=0))
            y_s[ci * c:(ci + 1) * c, ls] = yr_s[p] + jnp.where(lo_half, y0, y1)
            post(tok_row, y0)

        def group_norm(ci, tok_row=None):
            rows = slice(ci * c, (ci + 1) * c)
            segn = segn_ref[...]
            y = y_s[rows, :]
            yc = y - _seg_sum(y, segn)
            yn = yc * lax.rsqrt(_seg_sum(yc * yc, segn) + GN_EPS) * lnw_ref[...] + lnb_ref[...]
            y_ref[rows, :] = (yn + pre_f[rd, _PRE_BONUS, rows, :]) * pre_f[rd, _PRE_GATE, rows, :]
            post(tok_row, yn)

        def sweeps_of(chunks):
            qs = [(q, ci, p) for q, (ci, p) in enumerate(problems) if ci in chunks]
            out = [functools.partial(gram, q, ci, p) for q, ci, p in qs]
            out += [functools.partial(square, q) for q, _, _ in qs]
            for _ in range(levels - 2):
                out += [functools.partial(square_and_extend, q) for q, _, _ in qs]
            out += [functools.partial(extend_last, q) for q, _, _ in qs]
            out += [functools.partial(local_x, q, ci, p) for q, ci, p in qs]
            out += [functools.partial(apply_inverse, q, ci, p) for q, ci, p in qs]
            return out

        def serial_of(chunks):
            out = []
            for ci in chunks:
                out += [functools.partial(read_state, ci, p) for p in range(n_pairs)]
                out += [functools.partial(write_state, ci, p) for p in range(n_pairs)]
            return out

        first_half = list(range(n_chunks // 2))
        second_half = list(range(n_chunks // 2, n_chunks))
        norms = [functools.partial(group_norm, ci) for ci in range(n_chunks)]
        _interleave.next_row = None
        main = (sweeps_of(first_half)
                + _interleave(sweeps_of(second_half), serial_of(first_half))
                + _interleave(serial_of(second_half), norms[:-1]) + norms[-1:])
        _interleave.next_row = 0
        split = (len(main) * 2) // 3
        lead = (len(main) * 5) // 8
        tasks = ([prep_matmuls_1] + _interleave(main[:lead], prep_1) + main[lead:split]
                 + [prep_matmuls_2] + _interleave(main[split:], prep_2))
        for task in tasks:
            task()

    for parity in (0, 1):
        @pl.when(step % 2 == parity)
        def _(parity=parity):
            run(parity, 1 - parity)


def _rwkv_constants(tb, head):
    c = CHUNK
    i = np.arange(tb)
    tri = ((i[:, None] // c) == (i[None, :] // c)) & (i[None, :] <= i[:, None])
    lanes = np.arange(V7X_MXU_DIM) // head
    seg = (lanes[:, None] == lanes[None, :]).astype(np.float32)
    return jnp.asarray(seg, BF16), jnp.asarray(seg / head, BF16), jnp.asarray(tri, BF16)


def _rwkv(z, z_l, vfirst, prm, *, batch, seq, tb, r_dim, heads, head, first):
    m = batch * seq
    nt = seq // tb
    n_blocks = batch * nt
    d_lora = prm["w_up"].shape[0]
    l_dim = d_lora + prm["a_up"].shape[0]
    n_pairs = heads // 2
    seg, segn, tri = _rwkv_constants(tb, head)
    n_prob = (tb // CHUNK) * n_pairs
    c2 = 2 * CHUNK

    def prep_rows(width):
        return pl.BlockSpec((tb, width), lambda s: (jnp.minimum(s, n_blocks - 1), 0))

    def whole(arr):
        return pl.BlockSpec(arr.shape, lambda s: (0,) * arr.ndim, pipeline_mode=pl.Buffered(1))

    ops = [z, z_l]
    specs = [prep_rows(4 * r_dim), prep_rows(l_dim)]
    if not first:
        ops.append(vfirst)
        specs.append(prep_rows(r_dim))
    names = ["w0", "w_up", "a0", "a_up"]
    if not first:
        names += ["v0", "v_down", "v_up"]
    names += ["k_k", "k_a", "r_k", "ln_w", "ln_b"]
    consts = [prm[nm] for nm in names] + [seg, segn, tri]
    ops += consts
    specs += [whole(a) for a in consts]

    out_shape = [jax.ShapeDtypeStruct((m, r_dim), F32)]
    out_specs = [pl.BlockSpec((tb, r_dim), lambda s: (jnp.maximum(s - 1, 0), 0))]
    if first:
        out_shape.append(jax.ShapeDtypeStruct((m + tb, r_dim), F32))
        out_specs.append(pl.BlockSpec((tb, r_dim), lambda s: (s, 0)))

    scratch = [
        pltpu.VMEM((n_pairs, V7X_LANES, V7X_LANES), F32),
        pltpu.VMEM((2, 5, tb, r_dim), BF16),
        pltpu.VMEM((2, 2, tb, r_dim), F32),
        pltpu.VMEM((2, 8, r_dim), F32),
        pltpu.VMEM((7, tb, r_dim), F32),
        pltpu.VMEM((3, tb, r_dim), BF16),
        pltpu.VMEM((tb, r_dim), F32),
        pltpu.VMEM((n_prob, c2, 2 * c2), BF16),
        pltpu.VMEM((n_prob, c2, c2), F32),
        pltpu.VMEM((n_prob, c2, c2), BF16),
        pltpu.VMEM((n_prob, 2, CHUNK, c2), BF16),
        pltpu.VMEM((n_prob, c2, V7X_LANES), BF16),
        pltpu.VMEM((n_prob, c2, V7X_LANES), BF16),
        pltpu.VMEM((n_prob, c2, V7X_LANES), F32),
        pltpu.VMEM((n_pairs, CHUNK, V7X_LANES), BF16),
        pltpu.VMEM((n_pairs, CHUNK, V7X_LANES), F32),
        pltpu.VMEM((2 * n_prob, V7X_LANES), F32),
    ]
    res = pl.pallas_call(
        functools.partial(_rwkv_kernel, first=first, n_pairs=n_pairs, head=head, d_lora=d_lora, nt=nt),
        grid=(n_blocks + 1,),
        in_specs=specs,
        out_specs=out_specs,
        out_shape=out_shape,
        scratch_shapes=scratch,
        compiler_params=pltpu.CompilerParams(
            dimension_semantics=("arbitrary",), vmem_limit_bytes=V7X_VMEM_LIMIT_BYTES),
        name="rwkv_first" if first else "rwkv",
    )(*ops)
    return (res[0], res[1]) if first else (res[0], vfirst)


def _pool_kernel(z_ref, wp_ref, ps_ref, y_ref, carry_ref):
    tb = z_ref.shape[0]
    p_dim = ps_ref.shape[1]
    cg = p_dim // len(POOL_WINDOWS)
    ti = pl.program_id(1)

    @pl.when(ti == 0)
    def _():
        carry_ref[...] = jnp.zeros_like(carry_ref)

    u = z_ref[:, 0:p_dim]
    gp = z_ref[:, p_dim:2 * p_dim]
    ext = jnp.concatenate([carry_ref[...], u], axis=0)
    carry_ref[...] = z_ref[tb - MAX_POOL_WINDOW:tb, 0:p_dim]

    sums = {}
    s = ext
    span = 1
    while span < MAX_POOL_WINDOW:
        s = s + pltpu.roll(s, span, 0)
        span *= 2
        sums[span] = s

    pos = (ti * tb + 1 + lax.broadcasted_iota(jnp.int32, (tb, 1), 0)).astype(F32)
    outs = []
    for gi, win in enumerate(POOL_WINDOWS):
        cols = slice(gi * cg, (gi + 1) * cg)
        mean = sums[win][MAX_POOL_WINDOW:, cols] / jnp.minimum(pos, float(win))
        d = mean - u[:, cols]
        outs.append(_dot(d.astype(BF16), wp_ref[gi]))
    y = jnp.concatenate(outs, axis=1)
    y_ref[...] = y * ps_ref[...] * gp


def _pool(z, w_pool, pool_scale, *, batch, seq, tb, r_dim, pool_dim):
    nt = seq // tb
    return pl.pallas_call(
        _pool_kernel,
        grid=(batch, nt),
        in_specs=[
            pl.BlockSpec((tb, 2 * pool_dim), lambda bi, ti: (bi * nt + ti, (4 * r_dim) // (2 * pool_dim))),
            pl.BlockSpec(w_pool.shape, lambda bi, ti: (0, 0, 0)),
            pl.BlockSpec(pool_scale.shape, lambda bi, ti: (0, 0)),
        ],
        out_specs=pl.BlockSpec((tb, pool_dim), lambda bi, ti: (bi * nt + ti, 0)),
        out_shape=jax.ShapeDtypeStruct((batch * seq, pool_dim), F32),
        scratch_shapes=[pltpu.VMEM((MAX_POOL_WINDOW, pool_dim), F32)],
        compiler_params=pltpu.CompilerParams(
            dimension_semantics=("parallel", "arbitrary"), vmem_limit_bytes=V7X_VMEM_LIMIT_BYTES),
        name="pool",
    )(z, w_pool, pool_scale)


def _out_kernel(x_ref, yr_ref, yp_ref, p_ref, wo_ref, wpg_ref, bpg_ref, wple_ref, fg_ref, o_ref, *, last):
    r_dim = yr_ref.shape[1]
    x1 = (x_ref[...]
          + _dot(yr_ref[...].astype(BF16), wo_ref[0:r_dim, :])
          + _dot(yp_ref[...].astype(BF16), wo_ref[r_dim:, :]))
    gate = _sigmoid(_dot(x1.astype(BF16), wpg_ref[...]) + bpg_ref[...])
    x2 = x1 + gate * _dot(p_ref[...].astype(BF16), wple_ref[...])
    if last:
        ms = jnp.mean(x2 * x2, axis=-1, keepdims=True)
        x2 = x2 * lax.rsqrt(ms + RMS_EPS) * fg_ref[...]
    o_ref[...] = x2


def _out(x2, yr, yp, p3, layer, w_out, w_pg, b_pg, w_ple, final_g, *, tm, last):
    m, d = x2.shape

    def whole(arr):
        return pl.BlockSpec(arr.shape, lambda i: (0,) * arr.ndim, pipeline_mode=pl.Buffered(1))

    return pl.pallas_call(
        functools.partial(_out_kernel, last=last),
        grid=(m // tm,),
        in_specs=[
            pl.BlockSpec((tm, d), lambda i: (i, 0)),
            pl.BlockSpec((tm, yr.shape[1]), lambda i: (i, 0)),
            pl.BlockSpec((tm, yp.shape[1]), lambda i: (i, 0)),
            pl.BlockSpec((None, tm, p3.shape[2]), lambda i: (layer, i, 0)),
            whole(w_out), whole(w_pg), whole(b_pg), whole(w_ple), whole(final_g),
        ],
        out_specs=pl.BlockSpec((tm, d), lambda i: (i, 0)),
        out_shape=jax.ShapeDtypeStruct((m, d), F32),
        compiler_params=pltpu.CompilerParams(
            dimension_semantics=("parallel",), vmem_limit_bytes=V7X_VMEM_LIMIT_BYTES),
        name="out_proj",
    )(x2, yr, yp, p3, w_out, w_pg, b_pg, w_ple, final_g)


def _tile(n, cap):
    t = min(n, cap)
    assert n % t == 0
    return t


def kernel(x, p, norm_g, w_in, mu, w0, w_up, a0, a_up, v0, v_down, v_up, k_k, k_a, r_k, ln_w, ln_b, w_pool,
           pool_scale, w_out, w_ple, w_pg, b_pg, final_g):
    batch, seq, d = x.shape
    depth = w_in.shape[0]
    r_dim = w0.shape[1]
    heads, head = r_k.shape[1], r_k.shape[2]
    pool_dim = pool_scale.shape[1]
    n_shift = mu.shape[1]
    l_dim = n_shift - 3 * r_dim
    m = batch * seq
    assert 2 * head == V7X_LANES and heads % 2 == 0 and heads * head == r_dim and r_dim % V7X_MXU_DIM == 0
    assert l_dim % V7X_LANES == 0
    assert pool_dim == r_dim and pool_dim % len(POOL_WINDOWS) == 0

    tb_rwkv = _tile(seq, 256)
    tb_pool = _tile(seq, 512)
    tm_in = _tile(seq, 1024)
    tm_out = _tile(m, 256)
    assert tb_rwkv % CHUNK == 0 and tb_rwkv // CHUNK <= 8 and tb_pool >= MAX_POOL_WINDOW

    w_in_p = jnp.concatenate(
        [w_in[..., :3 * r_dim], w_in[..., n_shift:n_shift + r_dim], w_in[..., n_shift + r_dim:]],
        axis=-1).astype(BF16)
    w_in_l = w_in[..., 3 * r_dim:n_shift].astype(BF16)
    w_out_b = w_out.astype(BF16)
    w_pg_b = w_pg.astype(BF16)
    w_ple_b = w_ple.astype(BF16)
    w_pool_b = w_pool.astype(BF16)
    p3 = p.reshape(depth, m, p.shape[-1])
    x2 = x.reshape(m, d)
    fg = final_g.reshape(1, d)

    vfirst = None
    for i in range(depth):
        z, z_l = _in_proj(x2, norm_g[i].reshape(1, d), w_in_p[i], w_in_l[i], mu[i, :3 * r_dim].reshape(1, -1),
                          mu[i, 3 * r_dim:].reshape(1, -1), tm=tm_in, tn=r_dim, seq=seq, n_shift_tiles=3,
                          gate_tiles=(3, 5))
        prm = {
            "w0": w0[i].reshape(1, -1), "w_up": w_up[i].astype(BF16),
            "a0": a0[i].reshape(1, -1), "a_up": a_up[i].astype(BF16),
            "k_k": k_k[i].reshape(1, -1), "k_a": k_a[i].reshape(1, -1), "r_k": r_k[i].reshape(1, -1),
            "ln_w": ln_w[i].reshape(1, -1), "ln_b": ln_b[i].reshape(1, -1),
        }
        if i > 0:
            prm.update({"v0": v0[i - 1].reshape(1, -1), "v_down": v_down[i - 1].astype(BF16),
                        "v_up": v_up[i - 1].astype(BF16)})
        yr, vfirst = _rwkv(z, z_l, vfirst, prm, batch=batch, seq=seq, tb=tb_rwkv, r_dim=r_dim,
                           heads=heads, head=head, first=(i == 0))
        yp = _pool(z, w_pool_b[i], pool_scale[i].reshape(1, -1), batch=batch, seq=seq, tb=tb_pool,
                   r_dim=r_dim, pool_dim=pool_dim)
        x2 = _out(x2, yr, yp, p3, i, w_out_b[i], w_pg_b[i], b_pg[i].reshape(1, d), w_ple_b[i], fg,
                  tm=tm_out, last=(i == depth - 1))
    return x2.reshape(batch, seq, d)
```

```python
import functools
import math

import jax
import jax.numpy as jnp
import numpy as np
from jax import lax
from jax.experimental import pallas as pl
from jax.experimental.pallas import tpu as pltpu

F32 = jnp.float32
BF16 = jnp.bfloat16

RMS_EPS = 1e-6
GN_EPS = 64e-5
KK_EPS = 1e-12
POOL_WINDOWS = (2, 4, 8, 16)

V7X_LANES = 128
V7X_MXU_DIM = 256
V7X_VMEM_LIMIT_BYTES = 56 * 2**20
CHUNK = 64
MAX_POOL_WINDOW = max(POOL_WINDOWS)


def _dot(a, b):
    return jnp.dot(a, b, preferred_element_type=F32)


def _dot_nt(a, b):
    return lax.dot_general(a, b, (((1,), (1,)), ((), ())), preferred_element_type=F32)


def _dot_tn(a, b):
    return lax.dot_general(a, b, (((0,), (0,)), ((), ())), preferred_element_type=F32)


def _sigmoid(x):
    return 0.5 * jnp.tanh(0.5 * x) + 0.5


def _softplus(x):
    return jnp.maximum(x, 0.0) + jnp.log(1.0 + jnp.exp(-jnp.abs(x)))


def _split_bf16(x, parts):
    out = []
    for _ in range(parts):
        hi = x.astype(BF16)
        out.append(hi)
        x = x - hi.astype(F32)
    return out


def _dot_0_1(m01, x, parts):
    acc = None
    for part in _split_bf16(x, parts):
        d = _dot(m01, part)
        acc = d if acc is None else acc + d
    return acc


def _seg_sum(x, seg):
    w = seg.shape[0]
    xb = x.astype(BF16)
    return jnp.concatenate(
        [_dot(xb[:, q * w:(q + 1) * w], seg) for q in range(x.shape[1] // w)], axis=1)


def _in_proj_kernel(x_ref, g_ref, w_ref, wl_ref, mu_ref, mul_ref, o_ref, ol_ref, h_scr, carry_ref, carryl_ref, *,
                    n_shift_tiles, gate_tiles, blocks_per_seq, row_chunk):
    i = pl.program_id(0)
    j = pl.program_id(1)
    tm = x_ref.shape[0]
    n_rc = tm // row_chunk
    keep = (i % blocks_per_seq != 0).astype(F32)
    row = lax.broadcasted_iota(jnp.int32, (row_chunk, 1), 0)

    def rows(rc):
        return slice(rc * row_chunk, (rc + 1) * row_chunk)

    def shift_mix(zv, prev_row, mu):
        prev = jnp.where(row == 0, prev_row, pltpu.roll(zv, 1, 0))
        return zv + (prev - zv) * mu

    def normalize(rc):
        x = x_ref[rows(rc), :]
        ms = jnp.mean(x * x, axis=-1, keepdims=True)
        h = (x * lax.rsqrt(ms + RMS_EPS) * g_ref[...]).astype(BF16)
        h_scr[rows(rc), :] = h
        return h

    def shifted_tile(with_norm):
        prev = carry_ref[j, 0:1, :] * keep
        prev_l = carryl_ref[0:1, :] * keep
        for rc in range(n_rc):
            h = normalize(rc) if with_norm else h_scr[rows(rc), :]
            z = _dot(h, w_ref[...])
            o_ref[rows(rc), :] = shift_mix(z, prev, mu_ref[...])
            prev = z[row_chunk - 1:row_chunk]
            if with_norm:
                zl = _dot(h, wl_ref[...])
                ol_ref[rows(rc), :] = shift_mix(zl, prev_l, mul_ref[...])
                prev_l = zl[row_chunk - 1:row_chunk]
        carry_ref[j, 0:1, :] = prev
        if with_norm:
            carryl_ref[0:1, :] = prev_l

    @pl.when((i == 0) & (j == 0))
    def _():
        carry_ref[...] = jnp.zeros_like(carry_ref)
        carryl_ref[...] = jnp.zeros_like(carryl_ref)

    is_gate = functools.reduce(lambda a, b: a | b, [j == t for t in gate_tiles])

    @pl.when(j == 0)
    def _():
        shifted_tile(True)

    @pl.when((j > 0) & (j < n_shift_tiles))
    def _():
        shifted_tile(False)

    @pl.when(is_gate)
    def _():
        for rc in range(n_rc):
            z = _dot(h_scr[rows(rc), :], w_ref[...])
            o_ref[rows(rc), :] = z * _sigmoid(z)

    @pl.when((j >= n_shift_tiles) & jnp.logical_not(is_gate))
    def _():
        for rc in range(n_rc):
            o_ref[rows(rc), :] = _dot(h_scr[rows(rc), :], w_ref[...])


def _in_proj(x2, g, w, w_l, mu_main, mu_l, *, tm, tn, seq, n_shift_tiles, gate_tiles):
    m, d = x2.shape
    n = w.shape[1]
    n_l = w_l.shape[1]
    n_tiles = n // tn
    return pl.pallas_call(
        functools.partial(_in_proj_kernel, n_shift_tiles=n_shift_tiles, gate_tiles=gate_tiles,
                          blocks_per_seq=seq // tm, row_chunk=min(tm, V7X_MXU_DIM)),
        grid=(m // tm, n_tiles),
        in_specs=[
            pl.BlockSpec((tm, d), lambda i, j: (i, 0)),
            pl.BlockSpec((1, d), lambda i, j: (0, 0)),
            pl.BlockSpec((d, tn), lambda i, j: (0, j)),
            pl.BlockSpec((d, n_l), lambda i, j: (0, 0)),
            pl.BlockSpec((1, tn), lambda i, j: (0, jnp.minimum(j, n_shift_tiles - 1))),
            pl.BlockSpec((1, n_l), lambda i, j: (0, 0)),
        ],
        out_specs=[pl.BlockSpec((tm, tn), lambda i, j: (i, j)),
                   pl.BlockSpec((tm, n_l), lambda i, j: (i, 0))],
        out_shape=[jax.ShapeDtypeStruct((m, n), F32), jax.ShapeDtypeStruct((m, n_l), F32)],
        scratch_shapes=[pltpu.VMEM((tm, d), BF16),
                        pltpu.VMEM((n_shift_tiles, 8, tn), F32),
                        pltpu.VMEM((8, n_l), F32)],
        compiler_params=pltpu.CompilerParams(
            dimension_semantics=("arbitrary", "arbitrary"), vmem_limit_bytes=V7X_VMEM_LIMIT_BYTES),
        name="in_proj",
    )(x2, g, w, w_l, mu_main, mu_l)


def _rwkv_kernel(*refs, first, n_pairs, head, d_lora):
    it = iter(refs)
    z_ref, zl_ref = next(it), next(it)
    vf_ref = None if first else next(it)
    w0_ref, wup_ref, a0_ref, aup_ref = (next(it) for _ in range(4))
    if not first:
        v0_ref, vdn_ref, vup_ref = next(it), next(it), next(it)
    kk_ref, ka_ref, rk_ref, lnw_ref, lnb_ref, seg_ref, segn_ref, tri_ref = (next(it) for _ in range(8))
    y_ref = next(it)
    vfo_ref = next(it) if first else None
    (state_ref, at_s, rt_s, bt_s, kt_s, v_s, wc_s, y_s,
     ch_s, tinv_s, aak_s, ar_s, xloc_s, ta_s, uloc_s, u_s, yr_s) = it

    tb = z_ref.shape[0]
    r_dim = w0_ref.shape[1]
    c = CHUNK
    n_chunks = tb // c
    levels = int(math.log2(c))

    @pl.when(pl.program_id(1) == 0)
    def _():
        state_ref[...] = jnp.zeros_like(state_ref)

    r = z_ref[:, 0:r_dim]
    k = z_ref[:, r_dim:2 * r_dim]
    v = z_ref[:, 2 * r_dim:3 * r_dim]
    lora_in = zl_ref[...]

    wd = lora_in[:, 0:d_lora]
    ad = lora_in[:, d_lora:]
    w_loglog = -_softplus(-(w0_ref[...] + _dot(jnp.tanh(wd).astype(BF16), wup_ref[...]))) - 0.5
    logw = -jnp.exp(w_loglog)
    a = _sigmoid(a0_ref[...] + _dot(ad.astype(BF16), aup_ref[...]))

    if first:
        vfo_ref[...] = v
    else:
        low = _dot(v.astype(BF16), vdn_ref[...])
        nu = _sigmoid(v0_ref[...] + _dot(low.astype(BF16), vup_ref[...]))
        v = v + (vf_ref[...] - v) * nu

    seg = seg_ref[...]
    kk = k * kk_ref[...]
    kk = kk * lax.rsqrt(jnp.maximum(_seg_sum(kk * kk, seg), KK_EPS * KK_EPS))
    k = k * (1.0 + (a - 1.0) * ka_ref[...])
    b = kk * a

    cs = _dot_0_1(tri_ref[...], logw, 2)
    w_inv = jnp.exp(-cs)
    at_s[...] = (-kk * jnp.exp(cs - logw)).astype(BF16)
    rt_s[...] = (r * jnp.exp(cs)).astype(BF16)
    bt_s[...] = (b * w_inv).astype(BF16)
    kt_s[...] = (k * w_inv).astype(BF16)
    v_s[...] = v.astype(BF16)
    for i in range(n_chunks):
        wc_s[i:i + 1, :] = jnp.exp(cs[(i + 1) * c - 1:(i + 1) * c])

    bonus = _seg_sum(r * k * rk_ref[...], seg) * v
    gate = z_ref[:, 3 * r_dim:4 * r_dim]

    lane = lax.broadcasted_iota(jnp.int32, (c, V7X_LANES), 1)
    t_row = lax.broadcasted_iota(jnp.int32, (c, V7X_LANES), 0)
    s_col = lane % head
    lo_half = lane < head
    hi_half = lane >= head
    strict_lo = lo_half & (s_col < t_row)
    strict_hi = hi_half & (s_col < t_row)
    incl = s_col <= t_row
    eye = (lax.broadcasted_iota(jnp.int32, (2 * c, 2 * c), 0)
           == lax.broadcasted_iota(jnp.int32, (2 * c, 2 * c), 1)).astype(F32)
    same_head = (lax.broadcasted_iota(jnp.int32, (V7X_LANES, V7X_LANES), 0) // head
                 == lax.broadcasted_iota(jnp.int32, (V7X_LANES, V7X_LANES), 1) // head)

    def tile(src, ci, p):
        return src[ci * c:(ci + 1) * c, p * V7X_LANES:(p + 1) * V7X_LANES]

    def head_rows(xv):
        zero = jnp.zeros_like(xv)
        return jnp.concatenate([jnp.where(lo_half, xv, zero), jnp.where(lo_half, zero, xv)], axis=0)

    problems = [(ci, p) for ci in range(n_chunks) for p in range(n_pairs)]
    lw = V7X_LANES

    for q, (ci, p) in enumerate(problems):
        bt, kt = tile(bt_s, ci, p), tile(kt_s, ci, p)
        zero = jnp.zeros_like(bt)
        rhs = jnp.concatenate([jnp.where(lo_half, bt, zero), jnp.where(lo_half, kt, zero),
                               jnp.where(lo_half, zero, kt), jnp.where(lo_half, zero, bt)], axis=0)
        lhs = jnp.concatenate([tile(at_s, ci, p), tile(rt_s, ci, p)], axis=0)
        gram = _dot_nt(lhs, rhs)
        g0, g1 = gram[0:c, 0:lw], gram[0:c, lw:2 * lw]
        nmat = jnp.concatenate([jnp.where(strict_lo, g0, 0.0), jnp.where(strict_hi, g1, 0.0)], axis=0)
        aak_s[q] = jnp.concatenate([jnp.where(strict_hi, g0, 0.0), jnp.where(strict_lo, g1, 0.0)],
                                   axis=0).astype(BF16)
        ar_s[q, 0] = jnp.where(incl, gram[c:2 * c, 0:lw], 0.0).astype(BF16)
        ar_s[q, 1] = jnp.where(incl, gram[c:2 * c, lw:2 * lw], 0.0).astype(BF16)
        tinv = eye + nmat
        tinv_s[q] = tinv
        ch_s[q, :, 0:lw] = nmat.astype(BF16)
        ch_s[q, :, lw:2 * lw] = tinv.astype(BF16)
    for q in range(len(problems)):
        pw = ch_s[q, :, 0:lw]
        ch_s[q, :, 0:lw] = _dot(pw, pw).astype(BF16)
    for _ in range(levels - 2):
        for q in range(len(problems)):
            res = _dot(ch_s[q, :, 0:lw], ch_s[q])
            tinv = tinv_s[q] + res[:, lw:2 * lw]
            tinv_s[q] = tinv
            ch_s[q, :, 0:lw] = res[:, 0:lw].astype(BF16)
            ch_s[q, :, lw:2 * lw] = tinv.astype(BF16)
    for q in range(len(problems)):
        tinv = tinv_s[q] + _dot(ch_s[q, :, 0:lw], ch_s[q, :, lw:2 * lw])
        ch_s[q, :, lw:2 * lw] = tinv.astype(BF16)
    for q, (ci, p) in enumerate(problems):
        vt = tile(v_s, ci, p)
        xloc_s[q] = _dot(aak_s[q], jnp.concatenate([vt, vt], axis=0)).astype(BF16)
    for q, (ci, p) in enumerate(problems):
        tu = _dot(ch_s[q, :, lw:2 * lw], jnp.concatenate([head_rows(tile(at_s, ci, p)), xloc_s[q]], axis=1))
        ta_s[q] = tu[:, 0:lw].astype(BF16)
        uloc_s[q] = tu[:, lw:2 * lw]

    for ci in range(n_chunks):
        for p in range(n_pairs):
            q = ci * n_pairs + p
            lhs = jnp.concatenate([ta_s[q], tile(rt_s, ci, p)], axis=0)
            xh = _dot_nt(lhs, state_ref[p].astype(BF16))
            u2 = xh[0:2 * c] + uloc_s[q]
            u_s[p] = jnp.where(lo_half, u2[0:c], u2[c:2 * c]).astype(BF16)
            yr_s[p] = xh[2 * c:3 * c]
        for p in range(n_pairs):
            q = ci * n_pairs + p
            ls = slice(p * lw, (p + 1) * lw)
            ut, vt = u_s[p], tile(v_s, ci, p)
            uv = jnp.concatenate([ut, vt], axis=0)
            upd = _dot_tn(uv, jnp.concatenate([tile(bt_s, ci, p), tile(kt_s, ci, p)], axis=0))
            state_ref[p] = (state_ref[p] + jnp.where(same_head, upd, 0.0)) * wc_s[ci:ci + 1, ls]
            y0 = _dot(ar_s[q, 0], uv)
            y1 = _dot(ar_s[q, 1], jnp.concatenate([vt, ut], axis=0))
            y_s[ci * c:(ci + 1) * c, ls] = yr_s[p] + jnp.where(lo_half, y0, y1)

    y = y_s[...]
    segn = segn_ref[...]
    yc = y - _seg_sum(y, segn)
    yn = yc * lax.rsqrt(_seg_sum(yc * yc, segn) + GN_EPS) * lnw_ref[...] + lnb_ref[...]
    y_ref[...] = (yn + bonus) * gate


def _rwkv_constants(tb, head):
    c = CHUNK
    i = np.arange(tb)
    tri = ((i[:, None] // c) == (i[None, :] // c)) & (i[None, :] <= i[:, None])
    lanes = np.arange(V7X_MXU_DIM) // head
    seg = (lanes[:, None] == lanes[None, :]).astype(np.float32)
    return jnp.asarray(seg, BF16), jnp.asarray(seg / head, BF16), jnp.asarray(tri, BF16)


def _rwkv(z, z_l, vfirst, prm, *, batch, seq, tb, r_dim, heads, head, first):
    m = batch * seq
    nt = seq // tb
    d_lora = prm["w_up"].shape[0]
    l_dim = d_lora + prm["a_up"].shape[0]
    n_pairs = heads // 2
    seg, segn, tri = _rwkv_constants(tb, head)
    n_prob = (tb // CHUNK) * n_pairs
    c2 = 2 * CHUNK

    def rows(width, col):
        return pl.BlockSpec((tb, width), lambda bi, ti: (bi * nt + ti, col))

    def whole(arr):
        return pl.BlockSpec(arr.shape, lambda bi, ti: (0,) * arr.ndim, pipeline_mode=pl.Buffered(1))

    ops = [z, z_l]
    specs = [rows(4 * r_dim, 0), rows(l_dim, 0)]
    if not first:
        ops.append(vfirst)
        specs.append(rows(r_dim, 0))
    names = ["w0", "w_up", "a0", "a_up"]
    if not first:
        names += ["v0", "v_down", "v_up"]
    names += ["k_k", "k_a", "r_k", "ln_w", "ln_b"]
    consts = [prm[nm] for nm in names] + [seg, segn, tri]
    ops += consts
    specs += [whole(a) for a in consts]

    out_shape = [jax.ShapeDtypeStruct((m, r_dim), F32)]
    out_specs = [rows(r_dim, 0)]
    if first:
        out_shape.append(jax.ShapeDtypeStruct((m, r_dim), F32))
        out_specs.append(rows(r_dim, 0))

    scratch = [
        pltpu.VMEM((n_pairs, V7X_LANES, V7X_LANES), F32),
    ] + [pltpu.VMEM((tb, r_dim), BF16) for _ in range(5)] + [
        pltpu.VMEM((8, r_dim), F32),
        pltpu.VMEM((tb, r_dim), F32),
        pltpu.VMEM((n_prob, c2, 2 * c2), BF16),
        pltpu.VMEM((n_prob, c2, c2), F32),
        pltpu.VMEM((n_prob, c2, c2), BF16),
        pltpu.VMEM((n_prob, 2, CHUNK, c2), BF16),
        pltpu.VMEM((n_prob, c2, V7X_LANES), BF16),
        pltpu.VMEM((n_prob, c2, V7X_LANES), BF16),
        pltpu.VMEM((n_prob, c2, V7X_LANES), F32),
        pltpu.VMEM((n_pairs, CHUNK, V7X_LANES), BF16),
        pltpu.VMEM((n_pairs, CHUNK, V7X_LANES), F32),
    ]
    res = pl.pallas_call(
        functools.partial(_rwkv_kernel, first=first, n_pairs=n_pairs, head=head, d_lora=d_lora),
        grid=(batch, nt),
        in_specs=specs,
        out_specs=out_specs,
        out_shape=out_shape,
        scratch_shapes=scratch,
        compiler_params=pltpu.CompilerParams(
            dimension_semantics=("parallel", "arbitrary"), vmem_limit_bytes=V7X_VMEM_LIMIT_BYTES),
        name="rwkv_first" if first else "rwkv",
    )(*ops)
    return (res[0], res[1]) if first else (res[0], vfirst)


def _pool_mix(u, gp, carry, pos0, wp_ref, ps_ref):
    tb, p_dim = u.shape
    cg = p_dim // len(POOL_WINDOWS)
    ext = jnp.concatenate([carry, u], axis=0)
    sums = {}
    s = ext
    span = 1
    while span < MAX_POOL_WINDOW:
        s = s + pltpu.roll(s, span, 0)
        span *= 2
        sums[span] = s
    pos = (pos0 + 1 + lax.broadcasted_iota(jnp.int32, (tb, 1), 0)).astype(F32)
    outs = []
    for gi, win in enumerate(POOL_WINDOWS):
        cols = slice(gi * cg, (gi + 1) * cg)
        mean = sums[win][MAX_POOL_WINDOW:, cols] / jnp.minimum(pos, float(win))
        d = mean - u[:, cols]
        outs.append(_dot(d.astype(BF16), wp_ref[gi]))
    return jnp.concatenate(outs, axis=1) * ps_ref[...] * gp


def _out_kernel(x_ref, yr_ref, zp_ref, p_ref, wo_ref, wp_ref, ps_ref, wpg_ref, bpg_ref, wple_ref, fg_ref, o_ref,
                carry_ref, *, last, blocks_per_seq):
    i = pl.program_id(0)
    tm = x_ref.shape[0]
    r_dim = yr_ref.shape[1]
    p_dim = ps_ref.shape[1]

    @pl.when(i == 0)
    def _():
        carry_ref[...] = jnp.zeros_like(carry_ref)

    keep = (i % blocks_per_seq != 0).astype(F32)
    x1 = x_ref[...] + _dot(yr_ref[...].astype(BF16), wo_ref[0:r_dim, :])
    yp = _pool_mix(zp_ref[:, 0:p_dim], zp_ref[:, p_dim:2 * p_dim], carry_ref[...] * keep,
                   (i % blocks_per_seq) * tm, wp_ref, ps_ref)
    carry_ref[...] = zp_ref[tm - MAX_POOL_WINDOW:tm, 0:p_dim]
    x1 = x1 + _dot(yp.astype(BF16), wo_ref[r_dim:, :])
    gate = _sigmoid(_dot(x1.astype(BF16), wpg_ref[...]) + bpg_ref[...])
    x2 = x1 + gate * _dot(p_ref[...].astype(BF16), wple_ref[...])
    if last:
        ms = jnp.mean(x2 * x2, axis=-1, keepdims=True)
        x2 = x2 * lax.rsqrt(ms + RMS_EPS) * fg_ref[...]
    o_ref[...] = x2


def _out(x2, yr, z, p3, layer, w_out, w_pool, pool_scale, w_pg, b_pg, w_ple, final_g, *, tm, seq, last):
    m, d = x2.shape
    r_dim = yr.shape[1]
    pool_dim = pool_scale.shape[1]

    def whole(arr):
        return pl.BlockSpec(arr.shape, lambda i: (0,) * arr.ndim, pipeline_mode=pl.Buffered(1))

    return pl.pallas_call(
        functools.partial(_out_kernel, last=last, blocks_per_seq=seq // tm),
        grid=(m // tm,),
        in_specs=[
            pl.BlockSpec((tm, d), lambda i: (i, 0)),
            pl.BlockSpec((tm, r_dim), lambda i: (i, 0)),
            pl.BlockSpec((tm, 2 * pool_dim), lambda i: (i, (4 * r_dim) // (2 * pool_dim))),
            pl.BlockSpec((None, tm, p3.shape[2]), lambda i: (layer, i, 0)),
            whole(w_out), whole(w_pool), whole(pool_scale), whole(w_pg), whole(b_pg), whole(w_ple),
            whole(final_g),
        ],
        out_specs=pl.BlockSpec((tm, d), lambda i: (i, 0)),
        out_shape=jax.ShapeDtypeStruct((m, d), F32),
        scratch_shapes=[pltpu.VMEM((MAX_POOL_WINDOW, pool_dim), F32)],
        compiler_params=pltpu.CompilerParams(
            dimension_semantics=("arbitrary",), vmem_limit_bytes=V7X_VMEM_LIMIT_BYTES),
        name="out_proj",
    )(x2, yr, z, p3, w_out, w_pool, pool_scale, w_pg, b_pg, w_ple, final_g)


def _tile(n, cap):
    t = min(n, cap)
    assert n % t == 0
    return t


def kernel(x, p, norm_g, w_in, mu, w0, w_up, a0, a_up, v0, v_down, v_up, k_k, k_a, r_k, ln_w, ln_b, w_pool,
           pool_scale, w_out, w_ple, w_pg, b_pg, final_g):
    batch, seq, d = x.shape
    depth = w_in.shape[0]
    r_dim = w0.shape[1]
    heads, head = r_k.shape[1], r_k.shape[2]
    pool_dim = pool_scale.shape[1]
    n_shift = mu.shape[1]
    l_dim = n_shift - 3 * r_dim
    m = batch * seq
    assert 2 * head == V7X_LANES and heads % 2 == 0 and heads * head == r_dim and r_dim % V7X_MXU_DIM == 0
    assert l_dim % V7X_LANES == 0
    assert pool_dim == r_dim and pool_dim % len(POOL_WINDOWS) == 0

    tb_rwkv = _tile(seq, 256)
    tm_in = _tile(seq, 1024)
    tm_out = _tile(seq, 256)
    assert tb_rwkv % CHUNK == 0 and tb_rwkv // CHUNK <= 8 and tm_out >= MAX_POOL_WINDOW

    w_in_p = jnp.concatenate(
        [w_in[..., :3 * r_dim], w_in[..., n_shift:n_shift + r_dim], w_in[..., n_shift + r_dim:]],
        axis=-1).astype(BF16)
    w_in_l = w_in[..., 3 * r_dim:n_shift].astype(BF16)
    w_out_b = w_out.astype(BF16)
    w_pg_b = w_pg.astype(BF16)
    w_ple_b = w_ple.astype(BF16)
    w_pool_b = w_pool.astype(BF16)
    p3 = p.reshape(depth, m, p.shape[-1])
    x2 = x.reshape(m, d)
    fg = final_g.reshape(1, d)

    vfirst = None
    for i in range(depth):
        z, z_l = _in_proj(x2, norm_g[i].reshape(1, d), w_in_p[i], w_in_l[i], mu[i, :3 * r_dim].reshape(1, -1),
                          mu[i, 3 * r_dim:].reshape(1, -1), tm=tm_in, tn=r_dim, seq=seq, n_shift_tiles=3,
                          gate_tiles=(3, 5))
        prm = {
            "w0": w0[i].reshape(1, -1), "w_up": w_up[i].astype(BF16),
            "a0": a0[i].reshape(1, -1), "a_up": a_up[i].astype(BF16),
            "k_k": k_k[i].reshape(1, -1), "k_a": k_a[i].reshape(1, -1), "r_k": r_k[i].reshape(1, -1),
            "ln_w": ln_w[i].reshape(1, -1), "ln_b": ln_b[i].reshape(1, -1),
        }
        if i > 0:
            prm.update({"v0": v0[i - 1].reshape(1, -1), "v_down": v_down[i - 1].astype(BF16),
                        "v_up": v_up[i - 1].astype(BF16)})
        yr, vfirst = _rwkv(z, z_l, vfirst, prm, batch=batch, seq=seq, tb=tb_rwkv, r_dim=r_dim,
                           heads=heads, head=head, first=(i == 0))
        x2 = _out(x2, yr, z, p3, i, w_out_b[i], w_pool_b[i], pool_scale[i].reshape(1, -1), w_pg_b[i],
                  b_pg[i].reshape(1, d), w_ple_b[i], fg, tm=tm_out, seq=seq, last=(i == depth - 1))
    return x2.reshape(batch, seq, d)
```

```python
import functools
import math

import jax
import jax.numpy as jnp
import numpy as np
from jax import lax
from jax.experimental import pallas as pl
from jax.experimental.pallas import tpu as pltpu

F32 = jnp.float32
BF16 = jnp.bfloat16

RMS_EPS = 1e-6
GN_EPS = 64e-5
KK_EPS = 1e-12
POOL_WINDOWS = (2, 4, 8, 16)

V7X_LANES = 128
V7X_MXU_DIM = 256
V7X_VMEM_LIMIT_BYTES = 56 * 2**20
CHUNK = 64
SWEEP_GROUP = 2
MAX_POOL_WINDOW = max(POOL_WINDOWS)


def _dot(a, b):
    return jnp.dot(a, b, preferred_element_type=F32)


def _dot_nt(a, b):
    return lax.dot_general(a, b, (((1,), (1,)), ((), ())), preferred_element_type=F32)


def _dot_tn(a, b):
    return lax.dot_general(a, b, (((0,), (0,)), ((), ())), preferred_element_type=F32)


def _sigmoid(x):
    return 0.5 * jnp.tanh(0.5 * x) + 0.5


def _split_bf16(x, parts):
    out = []
    for _ in range(parts):
        hi = x.astype(BF16)
        out.append(hi)
        x = x - hi.astype(F32)
    return out


def _dot_0_1(m01, x, parts):
    acc = None
    for part in _split_bf16(x, parts):
        d = _dot(m01, part)
        acc = d if acc is None else acc + d
    return acc


def _seg_sum(x, seg):
    w = seg.shape[0]
    xb = x.astype(BF16)
    return jnp.concatenate(
        [_dot(xb[:, q * w:(q + 1) * w], seg) for q in range(x.shape[1] // w)], axis=1)


def _in_proj_kernel(x_ref, g_ref, w_ref, wl_ref, mu_ref, mul_ref, o_ref, ol_ref, h_scr, carry_ref, carryl_ref, *,
                    n_shift_tiles, gate_tiles, blocks_per_seq, row_chunk):
    i = pl.program_id(0)
    j = pl.program_id(1)
    tm = x_ref.shape[0]
    n_rc = tm // row_chunk
    keep = (i % blocks_per_seq != 0).astype(F32)
    row = lax.broadcasted_iota(jnp.int32, (row_chunk, 1), 0)

    def rows(rc):
        return slice(rc * row_chunk, (rc + 1) * row_chunk)

    def shift_mix(zv, prev_row, mu):
        prev = jnp.where(row == 0, prev_row, pltpu.roll(zv, 1, 0))
        return zv + (prev - zv) * mu

    def normalize(rc):
        x = x_ref[rows(rc), :]
        ms = jnp.mean(x * x, axis=-1, keepdims=True)
        h = (x * lax.rsqrt(ms + RMS_EPS) * g_ref[...]).astype(BF16)
        h_scr[rows(rc), :] = h
        return h

    def shifted_tile(with_norm):
        prev = carry_ref[j, 0:1, :] * keep
        prev_l = carryl_ref[0:1, :] * keep
        for rc in range(n_rc):
            h = normalize(rc) if with_norm else h_scr[rows(rc), :]
            z = _dot(h, w_ref[...])
            o_ref[rows(rc), :] = shift_mix(z, prev, mu_ref[...])
            prev = z[row_chunk - 1:row_chunk]
            if with_norm:
                zl = _dot(h, wl_ref[...])
                ol_ref[rows(rc), :] = shift_mix(zl, prev_l, mul_ref[...])
                prev_l = zl[row_chunk - 1:row_chunk]
        carry_ref[j, 0:1, :] = prev
        if with_norm:
            carryl_ref[0:1, :] = prev_l

    @pl.when((i == 0) & (j == 0))
    def _():
        carry_ref[...] = jnp.zeros_like(carry_ref)
        carryl_ref[...] = jnp.zeros_like(carryl_ref)

    is_gate = functools.reduce(lambda a, b: a | b, [j == t for t in gate_tiles])

    @pl.when(j == 0)
    def _():
        shifted_tile(True)

    @pl.when((j > 0) & (j < n_shift_tiles))
    def _():
        shifted_tile(False)

    @pl.when(is_gate)
    def _():
        for rc in range(n_rc):
            z = _dot(h_scr[rows(rc), :], w_ref[...])
            o_ref[rows(rc), :] = z * _sigmoid(z)

    @pl.when((j >= n_shift_tiles) & jnp.logical_not(is_gate))
    def _():
        for rc in range(n_rc):
            o_ref[rows(rc), :] = _dot(h_scr[rows(rc), :], w_ref[...])


def _in_proj(x2, g, w, w_l, mu_main, mu_l, *, tm, tn, seq, n_shift_tiles, gate_tiles):
    m, d = x2.shape
    n = w.shape[1]
    n_l = w_l.shape[1]
    n_tiles = n // tn
    return pl.pallas_call(
        functools.partial(_in_proj_kernel, n_shift_tiles=n_shift_tiles, gate_tiles=gate_tiles,
                          blocks_per_seq=seq // tm, row_chunk=min(tm, V7X_MXU_DIM)),
        grid=(m // tm, n_tiles),
        in_specs=[
            pl.BlockSpec((tm, d), lambda i, j: (i, 0)),
            pl.BlockSpec((1, d), lambda i, j: (0, 0)),
            pl.BlockSpec((d, tn), lambda i, j: (0, j)),
            pl.BlockSpec((d, n_l), lambda i, j: (0, 0)),
            pl.BlockSpec((1, tn), lambda i, j: (0, jnp.minimum(j, n_shift_tiles - 1))),
            pl.BlockSpec((1, n_l), lambda i, j: (0, 0)),
        ],
        out_specs=[pl.BlockSpec((tm, tn), lambda i, j: (i, j)),
                   pl.BlockSpec((tm, n_l), lambda i, j: (i, 0))],
        out_shape=[jax.ShapeDtypeStruct((m, n), F32), jax.ShapeDtypeStruct((m, n_l), F32)],
        scratch_shapes=[pltpu.VMEM((tm, d), BF16),
                        pltpu.VMEM((n_shift_tiles, 8, tn), F32),
                        pltpu.VMEM((8, n_l), F32)],
        compiler_params=pltpu.CompilerParams(
            dimension_semantics=("arbitrary", "arbitrary"), vmem_limit_bytes=V7X_VMEM_LIMIT_BYTES),
        name="in_proj",
    )(x2, g, w, w_l, mu_main, mu_l)


def _rwkv_kernel(*refs, first, n_pairs, head, d_lora):
    it = iter(refs)
    z_ref, zl_ref = next(it), next(it)
    vf_ref = None if first else next(it)
    w0_ref, wup_ref, a0_ref, aup_ref = (next(it) for _ in range(4))
    if not first:
        v0_ref, vdn_ref, vup_ref = next(it), next(it), next(it)
    kk_ref, ka_ref, rk_ref, lnw_ref, lnb_ref, seg_ref, segn_ref, tri_ref = (next(it) for _ in range(8))
    y_ref = next(it)
    vfo_ref = next(it) if first else None
    (state_ref, at_s, rt_s, bt_s, kt_s, v_s, wc_s, y_s,
     ch_s, tinv_s, aak_s, ar_s, xloc_s, ta_s, uloc_s, u_s, yr_s) = it

    tb = z_ref.shape[0]
    r_dim = w0_ref.shape[1]
    c = CHUNK
    n_chunks = tb // c
    levels = int(math.log2(c))

    @pl.when(pl.program_id(1) == 0)
    def _():
        state_ref[...] = jnp.zeros_like(state_ref)

    r = z_ref[:, 0:r_dim]
    k = z_ref[:, r_dim:2 * r_dim]
    v = z_ref[:, 2 * r_dim:3 * r_dim]
    lora_in = zl_ref[...]

    wd = lora_in[:, 0:d_lora]
    ad = lora_in[:, d_lora:]
    logw = -math.exp(-0.5) * _sigmoid(w0_ref[...] + _dot(jnp.tanh(wd).astype(BF16), wup_ref[...]))
    a = _sigmoid(a0_ref[...] + _dot(ad.astype(BF16), aup_ref[...]))

    if first:
        vfo_ref[...] = v
    else:
        low = _dot(v.astype(BF16), vdn_ref[...])
        nu = _sigmoid(v0_ref[...] + _dot(low.astype(BF16), vup_ref[...]))
        v = v + (vf_ref[...] - v) * nu

    seg = seg_ref[...]
    kk = k * kk_ref[...]
    kk = kk * lax.rsqrt(jnp.maximum(_seg_sum(kk * kk, seg), KK_EPS * KK_EPS))
    k = k * (1.0 + (a - 1.0) * ka_ref[...])
    b = kk * a

    cs = _dot_0_1(tri_ref[...], logw, 2)
    w_inv = jnp.exp(-cs)
    at_s[...] = (-kk * jnp.exp(cs - logw)).astype(BF16)
    rt_s[...] = (r * jnp.exp(cs)).astype(BF16)
    bt_s[...] = (b * w_inv).astype(BF16)
    kt_s[...] = (k * w_inv).astype(BF16)
    v_s[...] = v.astype(BF16)
    for i in range(n_chunks):
        wc_s[i:i + 1, :] = jnp.exp(cs[(i + 1) * c - 1:(i + 1) * c])

    bonus = _seg_sum(r * k * rk_ref[...], seg) * v
    gate = z_ref[:, 3 * r_dim:4 * r_dim]

    lane = lax.broadcasted_iota(jnp.int32, (c, V7X_LANES), 1)
    t_row = lax.broadcasted_iota(jnp.int32, (c, V7X_LANES), 0)
    s_col = lane % head
    lo_half = lane < head
    hi_half = lane >= head
    strict_lo = lo_half & (s_col < t_row)
    strict_hi = hi_half & (s_col < t_row)
    incl = s_col <= t_row
    eye = (lax.broadcasted_iota(jnp.int32, (2 * c, 2 * c), 0)
           == lax.broadcasted_iota(jnp.int32, (2 * c, 2 * c), 1)).astype(F32)
    same_head = (lax.broadcasted_iota(jnp.int32, (V7X_LANES, V7X_LANES), 0) // head
                 == lax.broadcasted_iota(jnp.int32, (V7X_LANES, V7X_LANES), 1) // head)

    def tile(src, ci, p):
        return src[ci * c:(ci + 1) * c, p * V7X_LANES:(p + 1) * V7X_LANES]

    def head_rows(xv):
        zero = jnp.zeros_like(xv)
        return jnp.concatenate([jnp.where(lo_half, xv, zero), jnp.where(lo_half, zero, xv)], axis=0)

    lw = V7X_LANES

    def gram(q, ci, p):
        bt, kt = tile(bt_s, ci, p), tile(kt_s, ci, p)
        zero = jnp.zeros_like(bt)
        rhs = jnp.concatenate([jnp.where(lo_half, bt, zero), jnp.where(lo_half, kt, zero),
                               jnp.where(lo_half, zero, kt), jnp.where(lo_half, zero, bt)], axis=0)
        lhs = jnp.concatenate([tile(at_s, ci, p), tile(rt_s, ci, p)], axis=0)
        gm = _dot_nt(lhs, rhs)
        g0, g1 = gm[0:c, 0:lw], gm[0:c, lw:2 * lw]
        nmat = jnp.concatenate([jnp.where(strict_lo, g0, 0.0), jnp.where(strict_hi, g1, 0.0)], axis=0)
        aak_s[q] = jnp.concatenate([jnp.where(strict_hi, g0, 0.0), jnp.where(strict_lo, g1, 0.0)],
                                   axis=0).astype(BF16)
        ar_s[q, 0] = jnp.where(incl, gm[c:2 * c, 0:lw], 0.0).astype(BF16)
        ar_s[q, 1] = jnp.where(incl, gm[c:2 * c, lw:2 * lw], 0.0).astype(BF16)
        tinv = eye + nmat
        tinv_s[q] = tinv
        ch_s[q, :, 0:lw] = nmat.astype(BF16)
        ch_s[q, :, lw:2 * lw] = tinv.astype(BF16)

    def square(q):
        pw = ch_s[q, :, 0:lw]
        ch_s[q, :, 0:lw] = _dot(pw, pw).astype(BF16)

    def square_and_extend(q):
        res = _dot(ch_s[q, :, 0:lw], ch_s[q])
        tinv = tinv_s[q] + res[:, lw:2 * lw]
        tinv_s[q] = tinv
        ch_s[q, :, 0:lw] = res[:, 0:lw].astype(BF16)
        ch_s[q, :, lw:2 * lw] = tinv.astype(BF16)

    def extend_last(q):
        tinv = tinv_s[q] + _dot(ch_s[q, :, 0:lw], ch_s[q, :, lw:2 * lw])
        ch_s[q, :, lw:2 * lw] = tinv.astype(BF16)

    def local_x(q, ci, p):
        vt = tile(v_s, ci, p)
        xloc_s[q] = _dot(aak_s[q], jnp.concatenate([vt, vt], axis=0)).astype(BF16)

    def apply_inverse(q, ci, p):
        tu = _dot(ch_s[q, :, lw:2 * lw], jnp.concatenate([head_rows(tile(at_s, ci, p)), xloc_s[q]], axis=1))
        ta_s[q] = tu[:, 0:lw].astype(BF16)
        uloc_s[q] = tu[:, lw:2 * lw]

    def sweep_stages(chunks):
        qs = [(ci * n_pairs + p, ci, p) for ci in chunks for p in range(n_pairs)]
        stages = [[functools.partial(gram, q, ci, p) for q, ci, p in qs],
                  [functools.partial(square, q) for q, _, _ in qs]]
        stages += [[functools.partial(square_and_extend, q) for q, _, _ in qs] for _ in range(levels - 2)]
        stages += [[functools.partial(extend_last, q) for q, _, _ in qs],
                   [functools.partial(local_x, q, ci, p) for q, ci, p in qs],
                   [functools.partial(apply_inverse, q, ci, p) for q, ci, p in qs]]
        return stages

    def read_state(ci):
        for p in range(n_pairs):
            q = ci * n_pairs + p
            lhs = jnp.concatenate([ta_s[q], tile(rt_s, ci, p)], axis=0)
            xh = _dot_nt(lhs, state_ref[p].astype(BF16))
            u2 = xh[0:2 * c] + uloc_s[q]
            u_s[p] = jnp.where(lo_half, u2[0:c], u2[c:2 * c]).astype(BF16)
            yr_s[p] = xh[2 * c:3 * c]

    def write_state(ci):
        for p in range(n_pairs):
            q = ci * n_pairs + p
            ls = slice(p * lw, (p + 1) * lw)
            ut, vt = u_s[p], tile(v_s, ci, p)
            uv = jnp.concatenate([ut, vt], axis=0)
            upd = _dot_tn(uv, jnp.concatenate([tile(bt_s, ci, p), tile(kt_s, ci, p)], axis=0))
            state_ref[p] = (state_ref[p] + jnp.where(same_head, upd, 0.0)) * wc_s[ci:ci + 1, ls]
            y0 = _dot(ar_s[q, 0], uv)
            y1 = _dot(ar_s[q, 1], jnp.concatenate([vt, ut], axis=0))
            y_s[ci * c:(ci + 1) * c, ls] = yr_s[p] + jnp.where(lo_half, y0, y1)

    groups = [list(range(g0, min(g0 + SWEEP_GROUP, n_chunks))) for g0 in range(0, n_chunks, SWEEP_GROUP)]
    prev = []
    for grp in groups:
        stages = sweep_stages(grp)
        riders = [functools.partial(f, ci) for ci in prev for f in (read_state, write_state)]
        done = 0
        for si, stage in enumerate(stages):
            for task in stage:
                task()
            want = (si + 1) * len(riders) // len(stages)
            for rider in riders[done:want]:
                rider()
            done = want
        prev = grp
    for ci in prev:
        read_state(ci)
        write_state(ci)

    y = y_s[...]
    segn = segn_ref[...]
    yc = y - _seg_sum(y, segn)
    yn = yc * lax.rsqrt(_seg_sum(yc * yc, segn) + GN_EPS) * lnw_ref[...] + lnb_ref[...]
    y_ref[...] = (yn + bonus) * gate


def _rwkv_constants(tb, head):
    c = CHUNK
    i = np.arange(tb)
    tri = ((i[:, None] // c) == (i[None, :] // c)) & (i[None, :] <= i[:, None])
    lanes = np.arange(V7X_MXU_DIM) // head
    seg = (lanes[:, None] == lanes[None, :]).astype(np.float32)
    return jnp.asarray(seg, BF16), jnp.asarray(seg / head, BF16), jnp.asarray(tri, BF16)


def _rwkv(z, z_l, vfirst, prm, *, batch, seq, tb, r_dim, heads, head, first):
    m = batch * seq
    nt = seq // tb
    d_lora = prm["w_up"].shape[0]
    l_dim = d_lora + prm["a_up"].shape[0]
    n_pairs = heads // 2
    seg, segn, tri = _rwkv_constants(tb, head)
    n_prob = (tb // CHUNK) * n_pairs
    c2 = 2 * CHUNK

    def rows(width, col):
        return pl.BlockSpec((tb, width), lambda bi, ti: (bi * nt + ti, col))

    def whole(arr):
        return pl.BlockSpec(arr.shape, lambda bi, ti: (0,) * arr.ndim, pipeline_mode=pl.Buffered(1))

    ops = [z, z_l]
    specs = [rows(4 * r_dim, 0), rows(l_dim, 0)]
    if not first:
        ops.append(vfirst)
        specs.append(rows(r_dim, 0))
    names = ["w0", "w_up", "a0", "a_up"]
    if not first:
        names += ["v0", "v_down", "v_up"]
    names += ["k_k", "k_a", "r_k", "ln_w", "ln_b"]
    consts = [prm[nm] for nm in names] + [seg, segn, tri]
    ops += consts
    specs += [whole(a) for a in consts]

    out_shape = [jax.ShapeDtypeStruct((m, r_dim), F32)]
    out_specs = [rows(r_dim, 0)]
    if first:
        out_shape.append(jax.ShapeDtypeStruct((m, r_dim), F32))
        out_specs.append(rows(r_dim, 0))

    scratch = [
        pltpu.VMEM((n_pairs, V7X_LANES, V7X_LANES), F32),
    ] + [pltpu.VMEM((tb, r_dim), BF16) for _ in range(5)] + [
        pltpu.VMEM((8, r_dim), F32),
        pltpu.VMEM((tb, r_dim), F32),
        pltpu.VMEM((n_prob, c2, 2 * c2), BF16),
        pltpu.VMEM((n_prob, c2, c2), F32),
        pltpu.VMEM((n_prob, c2, c2), BF16),
        pltpu.VMEM((n_prob, 2, CHUNK, c2), BF16),
        pltpu.VMEM((n_prob, c2, V7X_LANES), BF16),
        pltpu.VMEM((n_prob, c2, V7X_LANES), BF16),
        pltpu.VMEM((n_prob, c2, V7X_LANES), F32),
        pltpu.VMEM((n_pairs, CHUNK, V7X_LANES), BF16),
        pltpu.VMEM((n_pairs, CHUNK, V7X_LANES), F32),
    ]
    res = pl.pallas_call(
        functools.partial(_rwkv_kernel, first=first, n_pairs=n_pairs, head=head, d_lora=d_lora),
        grid=(batch, nt),
        in_specs=specs,
        out_specs=out_specs,
        out_shape=out_shape,
        scratch_shapes=scratch,
        compiler_params=pltpu.CompilerParams(
            dimension_semantics=("parallel", "arbitrary"), vmem_limit_bytes=V7X_VMEM_LIMIT_BYTES),
        name="rwkv_first" if first else "rwkv",
    )(*ops)
    return (res[0], res[1]) if first else (res[0], vfirst)


def _pool_mix(u, gp, carry, pos0, wp_ref, ps_ref):
    tb, p_dim = u.shape
    cg = p_dim // len(POOL_WINDOWS)
    ext = jnp.concatenate([carry, u], axis=0)
    sums = {}
    s = ext
    span = 1
    while span < MAX_POOL_WINDOW:
        s = s + pltpu.roll(s, span, 0)
        span *= 2
        sums[span] = s
    pos = (pos0 + 1 + lax.broadcasted_iota(jnp.int32, (tb, 1), 0)).astype(F32)
    outs = []
    for gi, win in enumerate(POOL_WINDOWS):
        cols = slice(gi * cg, (gi + 1) * cg)
        mean = sums[win][MAX_POOL_WINDOW:, cols] / jnp.minimum(pos, float(win))
        d = mean - u[:, cols]
        outs.append(_dot(d.astype(BF16), wp_ref[gi]))
    return jnp.concatenate(outs, axis=1) * ps_ref[...] * gp


def _out_kernel(x_ref, yr_ref, zp_ref, p_ref, wo_ref, wp_ref, ps_ref, wpg_ref, bpg_ref, wple_ref, fg_ref, o_ref,
                carry_ref, *, last, blocks_per_seq):
    i = pl.program_id(0)
    tm = x_ref.shape[0]
    r_dim = yr_ref.shape[1]
    p_dim = ps_ref.shape[1]

    @pl.when(i == 0)
    def _():
        carry_ref[...] = jnp.zeros_like(carry_ref)

    keep = (i % blocks_per_seq != 0).astype(F32)
    x1 = x_ref[...] + _dot(yr_ref[...].astype(BF16), wo_ref[0:r_dim, :])
    yp = _pool_mix(zp_ref[:, 0:p_dim], zp_ref[:, p_dim:2 * p_dim], carry_ref[...] * keep,
                   (i % blocks_per_seq) * tm, wp_ref, ps_ref)
    carry_ref[...] = zp_ref[tm - MAX_POOL_WINDOW:tm, 0:p_dim]
    x1 = x1 + _dot(yp.astype(BF16), wo_ref[r_dim:, :])
    gate = _sigmoid(_dot(x1.astype(BF16), wpg_ref[...]) + bpg_ref[...])
    x2 = x1 + gate * _dot(p_ref[...].astype(BF16), wple_ref[...])
    if last:
        ms = jnp.mean(x2 * x2, axis=-1, keepdims=True)
        x2 = x2 * lax.rsqrt(ms + RMS_EPS) * fg_ref[...]
    o_ref[...] = x2


def _out(x2, yr, z, p3, layer, w_out, w_pool, pool_scale, w_pg, b_pg, w_ple, final_g, *, tm, seq, last):
    m, d = x2.shape
    r_dim = yr.shape[1]
    pool_dim = pool_scale.shape[1]

    def whole(arr):
        return pl.BlockSpec(arr.shape, lambda i: (0,) * arr.ndim, pipeline_mode=pl.Buffered(1))

    return pl.pallas_call(
        functools.partial(_out_kernel, last=last, blocks_per_seq=seq // tm),
        grid=(m // tm,),
        in_specs=[
            pl.BlockSpec((tm, d), lambda i: (i, 0)),
            pl.BlockSpec((tm, r_dim), lambda i: (i, 0)),
            pl.BlockSpec((tm, 2 * pool_dim), lambda i: (i, (4 * r_dim) // (2 * pool_dim))),
            pl.BlockSpec((None, tm, p3.shape[2]), lambda i: (layer, i, 0)),
            whole(w_out), whole(w_pool), whole(pool_scale), whole(w_pg), whole(b_pg), whole(w_ple),
            whole(final_g),
        ],
        out_specs=pl.BlockSpec((tm, d), lambda i: (i, 0)),
        out_shape=jax.ShapeDtypeStruct((m, d), F32),
        scratch_shapes=[pltpu.VMEM((MAX_POOL_WINDOW, pool_dim), F32)],
        compiler_params=pltpu.CompilerParams(
            dimension_semantics=("arbitrary",), vmem_limit_bytes=V7X_VMEM_LIMIT_BYTES),
        name="out_proj",
    )(x2, yr, z, p3, w_out, w_pool, pool_scale, w_pg, b_pg, w_ple, final_g)


def _tile(n, cap):
    t = min(n, cap)
    assert n % t == 0
    return t


def kernel(x, p, norm_g, w_in, mu, w0, w_up, a0, a_up, v0, v_down, v_up, k_k, k_a, r_k, ln_w, ln_b, w_pool,
           pool_scale, w_out, w_ple, w_pg, b_pg, final_g):
    batch, seq, d = x.shape
    depth = w_in.shape[0]
    r_dim = w0.shape[1]
    heads, head = r_k.shape[1], r_k.shape[2]
    pool_dim = pool_scale.shape[1]
    n_shift = mu.shape[1]
    l_dim = n_shift - 3 * r_dim
    m = batch * seq
    assert 2 * head == V7X_LANES and heads % 2 == 0 and heads * head == r_dim and r_dim % V7X_MXU_DIM == 0
    assert l_dim % V7X_LANES == 0
    assert pool_dim == r_dim and pool_dim % len(POOL_WINDOWS) == 0

    tb_rwkv = _tile(seq, 256)
    tm_in = _tile(seq, 1024)
    tm_out = _tile(seq, 256)
    assert tb_rwkv % CHUNK == 0 and tb_rwkv // CHUNK <= 8 and tm_out >= MAX_POOL_WINDOW

    w_in_p = jnp.concatenate(
        [w_in[..., :3 * r_dim], w_in[..., n_shift:n_shift + r_dim], w_in[..., n_shift + r_dim:]],
        axis=-1).astype(BF16)
    w_in_l = w_in[..., 3 * r_dim:n_shift].astype(BF16)
    w_out_b = w_out.astype(BF16)
    w_pg_b = w_pg.astype(BF16)
    w_ple_b = w_ple.astype(BF16)
    w_pool_b = w_pool.astype(BF16)
    p3 = p.reshape(depth, m, p.shape[-1])
    x2 = x.reshape(m, d)
    fg = final_g.reshape(1, d)

    vfirst = None
    for i in range(depth):
        z, z_l = _in_proj(x2, norm_g[i].reshape(1, d), w_in_p[i], w_in_l[i], mu[i, :3 * r_dim].reshape(1, -1),
                          mu[i, 3 * r_dim:].reshape(1, -1), tm=tm_in, tn=r_dim, seq=seq, n_shift_tiles=3,
                          gate_tiles=(3, 5))
        prm = {
            "w0": w0[i].reshape(1, -1), "w_up": w_up[i].astype(BF16),
            "a0": a0[i].reshape(1, -1), "a_up": a_up[i].astype(BF16),
            "k_k": k_k[i].reshape(1, -1), "k_a": k_a[i].reshape(1, -1), "r_k": r_k[i].reshape(1, -1),
            "ln_w": ln_w[i].reshape(1, -1), "ln_b": ln_b[i].reshape(1, -1),
        }
        if i > 0:
            prm.update({"v0": v0[i - 1].reshape(1, -1), "v_down": v_down[i - 1].astype(BF16),
                        "v_up": v_up[i - 1].astype(BF16)})
        yr, vfirst = _rwkv(z, z_l, vfirst, prm, batch=batch, seq=seq, tb=tb_rwkv, r_dim=r_dim,
                           heads=heads, head=head, first=(i == 0))
        x2 = _out(x2, yr, z, p3, i, w_out_b[i], w_pool_b[i], pool_scale[i].reshape(1, -1), w_pg_b[i],
                  b_pg[i].reshape(1, d), w_ple_b[i], fg, tm=tm_out, seq=seq, last=(i == depth - 1))
    return x2.reshape(batch, seq, d)
```

```python
import functools
import math

import jax
import jax.numpy as jnp
import numpy as np
from jax import lax
from jax.experimental import pallas as pl
from jax.experimental.pallas import tpu as pltpu

F32 = jnp.float32
BF16 = jnp.bfloat16

RMS_EPS = 1e-6
GN_EPS = 64e-5
KK_EPS = 1e-12
POOL_WINDOWS = (2, 4, 8, 16)

V7X_LANES = 128
V7X_MXU_DIM = 256
V7X_VMEM_LIMIT_BYTES = 56 * 2**20
CHUNK = 64
SWEEP_GROUP = 2
MAX_POOL_WINDOW = max(POOL_WINDOWS)


def _dot(a, b):
    return jnp.dot(a, b, preferred_element_type=F32)


def _dot_nt(a, b):
    return lax.dot_general(a, b, (((1,), (1,)), ((), ())), preferred_element_type=F32)


def _dot_tn(a, b):
    return lax.dot_general(a, b, (((0,), (0,)), ((), ())), preferred_element_type=F32)


def _sigmoid(x):
    return 0.5 * jnp.tanh(0.5 * x) + 0.5


def _split_bf16(x, parts):
    out = []
    for _ in range(parts):
        hi = x.astype(BF16)
        out.append(hi)
        x = x - hi.astype(F32)
    return out


def _dot_0_1(m01, x, parts):
    acc = None
    for part in _split_bf16(x, parts):
        d = _dot(m01, part)
        acc = d if acc is None else acc + d
    return acc


def _seg_sum(x, seg):
    w = seg.shape[0]
    xb = x.astype(BF16)
    return jnp.concatenate(
        [_dot(xb[:, q * w:(q + 1) * w], seg) for q in range(x.shape[1] // w)], axis=1)


def _in_proj_kernel(x_ref, g_ref, w_ref, wl_ref, mu_ref, mul_ref, o_ref, ol_ref, h_scr, carry_ref, carryl_ref, *,
                    n_shift_tiles, gate_tiles, blocks_per_seq, row_chunk):
    i = pl.program_id(0)
    j = pl.program_id(1)
    tm = x_ref.shape[0]
    n_rc = tm // row_chunk
    keep = (i % blocks_per_seq != 0).astype(F32)
    row = lax.broadcasted_iota(jnp.int32, (row_chunk, 1), 0)

    def rows(rc):
        return slice(rc * row_chunk, (rc + 1) * row_chunk)

    def shift_mix(zv, prev_row, mu):
        prev = jnp.where(row == 0, prev_row, pltpu.roll(zv, 1, 0))
        return zv + (prev - zv) * mu

    def normalize(rc):
        x = x_ref[rows(rc), :]
        ms = jnp.mean(x * x, axis=-1, keepdims=True)
        h = (x * lax.rsqrt(ms + RMS_EPS) * g_ref[...]).astype(BF16)
        h_scr[rows(rc), :] = h
        return h

    def shifted_tile(with_norm):
        prev = carry_ref[j, 0:1, :] * keep
        prev_l = carryl_ref[0:1, :] * keep
        for rc in range(n_rc):
            h = normalize(rc) if with_norm else h_scr[rows(rc), :]
            z = _dot(h, w_ref[...])
            o_ref[rows(rc), :] = shift_mix(z, prev, mu_ref[...])
            prev = z[row_chunk - 1:row_chunk]
            if with_norm:
                zl = _dot(h, wl_ref[...])
                ol_ref[rows(rc), :] = shift_mix(zl, prev_l, mul_ref[...])
                prev_l = zl[row_chunk - 1:row_chunk]
        carry_ref[j, 0:1, :] = prev
        if with_norm:
            carryl_ref[0:1, :] = prev_l

    @pl.when((i == 0) & (j == 0))
    def _():
        carry_ref[...] = jnp.zeros_like(carry_ref)
        carryl_ref[...] = jnp.zeros_like(carryl_ref)

    is_gate = functools.reduce(lambda a, b: a | b, [j == t for t in gate_tiles])

    @pl.when(j == 0)
    def _():
        shifted_tile(True)

    @pl.when((j > 0) & (j < n_shift_tiles))
    def _():
        shifted_tile(False)

    @pl.when(is_gate)
    def _():
        for rc in range(n_rc):
            z = _dot(h_scr[rows(rc), :], w_ref[...])
            o_ref[rows(rc), :] = z * _sigmoid(z)

    @pl.when((j >= n_shift_tiles) & jnp.logical_not(is_gate))
    def _():
        for rc in range(n_rc):
            o_ref[rows(rc), :] = _dot(h_scr[rows(rc), :], w_ref[...])


def _in_proj(x2, g, w, w_l, mu_main, mu_l, *, tm, tn, seq, n_shift_tiles, gate_tiles):
    m, d = x2.shape
    n_tiles = w.shape[0]
    assert w.shape[2] == tn
    n_l = w_l.shape[1]
    return pl.pallas_call(
        functools.partial(_in_proj_kernel, n_shift_tiles=n_shift_tiles, gate_tiles=gate_tiles,
                          blocks_per_seq=seq // tm, row_chunk=min(tm, V7X_MXU_DIM)),
        grid=(m // tm, n_tiles),
        in_specs=[
            pl.BlockSpec((tm, d), lambda i, j: (i, 0)),
            pl.BlockSpec((1, d), lambda i, j: (0, 0)),
            pl.BlockSpec((None, d, tn), lambda i, j: (j, 0, 0)),
            pl.BlockSpec((d, n_l), lambda i, j: (0, 0)),
            pl.BlockSpec((1, tn), lambda i, j: (0, jnp.minimum(j, n_shift_tiles - 1))),
            pl.BlockSpec((1, n_l), lambda i, j: (0, 0)),
        ],
        out_specs=[pl.BlockSpec((None, tm, tn), lambda i, j: (j, i, 0)),
                   pl.BlockSpec((tm, n_l), lambda i, j: (i, 0))],
        out_shape=[jax.ShapeDtypeStruct((n_tiles, m, tn), F32), jax.ShapeDtypeStruct((m, n_l), F32)],
        scratch_shapes=[pltpu.VMEM((tm, d), BF16),
                        pltpu.VMEM((n_shift_tiles, 8, tn), F32),
                        pltpu.VMEM((8, n_l), F32)],
        compiler_params=pltpu.CompilerParams(
            dimension_semantics=("arbitrary", "arbitrary"), vmem_limit_bytes=V7X_VMEM_LIMIT_BYTES),
        name="in_proj",
    )(x2, g, w, w_l, mu_main, mu_l)


def _rwkv_kernel(*refs, first, n_pairs, head, d_lora):
    it = iter(refs)
    r_ref, k_ref, v_ref, gate_ref, zl_ref = (next(it) for _ in range(5))
    vf_ref = None if first else next(it)
    w0_ref, wup_ref, a0_ref, aup_ref = (next(it) for _ in range(4))
    if not first:
        v0_ref, vdn_ref, vup_ref = next(it), next(it), next(it)
    kk_ref, ka_ref, rk_ref, lnw_ref, lnb_ref, seg_ref, segn_ref, tri_ref = (next(it) for _ in range(8))
    y_ref = next(it)
    vfo_ref = next(it) if first else None
    (state_ref, at_s, rt_s, bt_s, kt_s, v_s, wc_s, y_s,
     ch_s, tinv_s, aak_s, ar_s, xloc_s, ta_s, uloc_s, u_s, yr_s) = it

    tb = r_ref.shape[0]
    r_dim = w0_ref.shape[1]
    c = CHUNK
    n_chunks = tb // c
    levels = int(math.log2(c))

    @pl.when(pl.program_id(1) == 0)
    def _():
        state_ref[...] = jnp.zeros_like(state_ref)

    r = r_ref[...]
    k = k_ref[...]
    v = v_ref[...]
    lora_in = zl_ref[...]

    wd = lora_in[:, 0:d_lora]
    ad = lora_in[:, d_lora:]
    logw = -math.exp(-0.5) * _sigmoid(w0_ref[...] + _dot(jnp.tanh(wd).astype(BF16), wup_ref[...]))
    a = _sigmoid(a0_ref[...] + _dot(ad.astype(BF16), aup_ref[...]))

    if first:
        vfo_ref[...] = v
    else:
        low = _dot(v.astype(BF16), vdn_ref[...])
        nu = _sigmoid(v0_ref[...] + _dot(low.astype(BF16), vup_ref[...]))
        v = v + (vf_ref[...] - v) * nu

    seg = seg_ref[...]
    kk = k * kk_ref[...]
    kk = kk * lax.rsqrt(jnp.maximum(_seg_sum(kk * kk, seg), KK_EPS * KK_EPS))
    k = k * (1.0 + (a - 1.0) * ka_ref[...])
    b = kk * a

    cs = _dot_0_1(tri_ref[...], logw, 2)
    w_inv = jnp.exp(-cs)
    at_s[...] = (-kk * jnp.exp(cs - logw)).astype(BF16)
    rt_s[...] = (r * jnp.exp(cs)).astype(BF16)
    bt_s[...] = (b * w_inv).astype(BF16)
    kt_s[...] = (k * w_inv).astype(BF16)
    v_s[...] = v.astype(BF16)
    for i in range(n_chunks):
        wc_s[i:i + 1, :] = jnp.exp(cs[(i + 1) * c - 1:(i + 1) * c])

    bonus = _seg_sum(r * k * rk_ref[...], seg) * v
    gate = gate_ref[...]

    lane = lax.broadcasted_iota(jnp.int32, (c, V7X_LANES), 1)
    t_row = lax.broadcasted_iota(jnp.int32, (c, V7X_LANES), 0)
    s_col = lane % head
    lo_half = lane < head
    hi_half = lane >= head
    strict_lo = lo_half & (s_col < t_row)
    strict_hi = hi_half & (s_col < t_row)
    incl = s_col <= t_row
    eye = (lax.broadcasted_iota(jnp.int32, (2 * c, 2 * c), 0)
           == lax.broadcasted_iota(jnp.int32, (2 * c, 2 * c), 1)).astype(F32)
    same_head = (lax.broadcasted_iota(jnp.int32, (V7X_LANES, V7X_LANES), 0) // head
                 == lax.broadcasted_iota(jnp.int32, (V7X_LANES, V7X_LANES), 1) // head)

    def tile(src, ci, p):
        return src[ci * c:(ci + 1) * c, p * V7X_LANES:(p + 1) * V7X_LANES]

    def head_rows(xv):
        zero = jnp.zeros_like(xv)
        return jnp.concatenate([jnp.where(lo_half, xv, zero), jnp.where(lo_half, zero, xv)], axis=0)

    lw = V7X_LANES

    def gram(q, ci, p):
        bt, kt = tile(bt_s, ci, p), tile(kt_s, ci, p)
        zero = jnp.zeros_like(bt)
        rhs = jnp.concatenate([jnp.where(lo_half, bt, zero), jnp.where(lo_half, kt, zero),
                               jnp.where(lo_half, zero, kt), jnp.where(lo_half, zero, bt)], axis=0)
        lhs = jnp.concatenate([tile(at_s, ci, p), tile(rt_s, ci, p)], axis=0)
        gm = _dot_nt(lhs, rhs)
        g0, g1 = gm[0:c, 0:lw], gm[0:c, lw:2 * lw]
        nmat = jnp.concatenate([jnp.where(strict_lo, g0, 0.0), jnp.where(strict_hi, g1, 0.0)], axis=0)
        aak_s[q] = jnp.concatenate([jnp.where(strict_hi, g0, 0.0), jnp.where(strict_lo, g1, 0.0)],
                                   axis=0).astype(BF16)
        ar_s[q, 0] = jnp.where(incl, gm[c:2 * c, 0:lw], 0.0).astype(BF16)
        ar_s[q, 1] = jnp.where(incl, gm[c:2 * c, lw:2 * lw], 0.0).astype(BF16)
        tinv = eye + nmat
        tinv_s[q] = tinv
        ch_s[q, :, 0:lw] = nmat.astype(BF16)
        ch_s[q, :, lw:2 * lw] = tinv.astype(BF16)

    def square(q):
        pw = ch_s[q, :, 0:lw]
        ch_s[q, :, 0:lw] = _dot(pw, pw).astype(BF16)

    def square_and_extend(q):
        res = _dot(ch_s[q, :, 0:lw], ch_s[q])
        tinv = tinv_s[q] + res[:, lw:2 * lw]
        tinv_s[q] = tinv
        ch_s[q, :, 0:lw] = res[:, 0:lw].astype(BF16)
        ch_s[q, :, lw:2 * lw] = tinv.astype(BF16)

    def extend_last(q):
        tinv = tinv_s[q] + _dot(ch_s[q, :, 0:lw], ch_s[q, :, lw:2 * lw])
        ch_s[q, :, lw:2 * lw] = tinv.astype(BF16)

    def local_x(q, ci, p):
        vt = tile(v_s, ci, p)
        xloc_s[q] = _dot(aak_s[q], jnp.concatenate([vt, vt], axis=0)).astype(BF16)

    def apply_inverse(q, ci, p):
        tu = _dot(ch_s[q, :, lw:2 * lw], jnp.concatenate([head_rows(tile(at_s, ci, p)), xloc_s[q]], axis=1))
        ta_s[q] = tu[:, 0:lw].astype(BF16)
        uloc_s[q] = tu[:, lw:2 * lw]

    def sweep_stages(chunks):
        qs = [(ci * n_pairs + p, ci, p) for ci in chunks for p in range(n_pairs)]
        stages = [[functools.partial(gram, q, ci, p) for q, ci, p in qs],
                  [functools.partial(square, q) for q, _, _ in qs]]
        stages += [[functools.partial(square_and_extend, q) for q, _, _ in qs] for _ in range(levels - 2)]
        stages += [[functools.partial(extend_last, q) for q, _, _ in qs],
                   [functools.partial(local_x, q, ci, p) for q, ci, p in qs],
                   [functools.partial(apply_inverse, q, ci, p) for q, ci, p in qs]]
        return stages

    def read_state(ci):
        for p in range(n_pairs):
            q = ci * n_pairs + p
            lhs = jnp.concatenate([ta_s[q], tile(rt_s, ci, p)], axis=0)
            xh = _dot_nt(lhs, state_ref[p].astype(BF16))
            u2 = xh[0:2 * c] + uloc_s[q]
            u_s[p] = jnp.where(lo_half, u2[0:c], u2[c:2 * c]).astype(BF16)
            yr_s[p] = xh[2 * c:3 * c]

    def write_state(ci):
        for p in range(n_pairs):
            q = ci * n_pairs + p
            ls = slice(p * lw, (p + 1) * lw)
            ut, vt = u_s[p], tile(v_s, ci, p)
            uv = jnp.concatenate([ut, vt], axis=0)
            upd = _dot_tn(uv, jnp.concatenate([tile(bt_s, ci, p), tile(kt_s, ci, p)], axis=0))
            state_ref[p] = (state_ref[p] + jnp.where(same_head, upd, 0.0)) * wc_s[ci:ci + 1, ls]
            y0 = _dot(ar_s[q, 0], uv)
            y1 = _dot(ar_s[q, 1], jnp.concatenate([vt, ut], axis=0))
            y_s[ci * c:(ci + 1) * c, ls] = yr_s[p] + jnp.where(lo_half, y0, y1)

    groups = [list(range(g0, min(g0 + SWEEP_GROUP, n_chunks))) for g0 in range(0, n_chunks, SWEEP_GROUP)]
    prev = []
    for grp in groups:
        stages = sweep_stages(grp)
        riders = [functools.partial(f, ci) for ci in prev for f in (read_state, write_state)]
        done = 0
        for si, stage in enumerate(stages):
            for task in stage:
                task()
            want = (si + 1) * len(riders) // len(stages)
            for rider in riders[done:want]:
                rider()
            done = want
        prev = grp
    for ci in prev:
        read_state(ci)
        write_state(ci)

    y = y_s[...]
    segn = segn_ref[...]
    yc = y - _seg_sum(y, segn)
    yn = yc * lax.rsqrt(_seg_sum(yc * yc, segn) + GN_EPS) * lnw_ref[...] + lnb_ref[...]
    y_ref[...] = (yn + bonus) * gate


def _rwkv_constants(tb, head):
    c = CHUNK
    i = np.arange(tb)
    tri = ((i[:, None] // c) == (i[None, :] // c)) & (i[None, :] <= i[:, None])
    lanes = np.arange(V7X_MXU_DIM) // head
    seg = (lanes[:, None] == lanes[None, :]).astype(np.float32)
    return jnp.asarray(seg, BF16), jnp.asarray(seg / head, BF16), jnp.asarray(tri, BF16)


def _rwkv(z, z_l, vfirst, prm, *, batch, seq, tb, r_dim, heads, head, first):
    m = batch * seq
    nt = seq // tb
    d_lora = prm["w_up"].shape[0]
    l_dim = d_lora + prm["a_up"].shape[0]
    n_pairs = heads // 2
    seg, segn, tri = _rwkv_constants(tb, head)
    n_prob = (tb // CHUNK) * n_pairs
    c2 = 2 * CHUNK

    def rows(width, col):
        return pl.BlockSpec((tb, width), lambda bi, ti: (bi * nt + ti, col))

    def whole(arr):
        return pl.BlockSpec(arr.shape, lambda bi, ti: (0,) * arr.ndim, pipeline_mode=pl.Buffered(1))

    def z_tile(t):
        return pl.BlockSpec((None, tb, r_dim), lambda bi, ti: (t, bi * nt + ti, 0))

    ops = [z, z, z, z, z_l]
    specs = [z_tile(0), z_tile(1), z_tile(2), z_tile(3), rows(l_dim, 0)]
    if not first:
        ops.append(vfirst)
        specs.append(rows(r_dim, 0))
    names = ["w0", "w_up", "a0", "a_up"]
    if not first:
        names += ["v0", "v_down", "v_up"]
    names += ["k_k", "k_a", "r_k", "ln_w", "ln_b"]
    consts = [prm[nm] for nm in names] + [seg, segn, tri]
    ops += consts
    specs += [whole(a) for a in consts]

    out_shape = [jax.ShapeDtypeStruct((m, r_dim), F32)]
    out_specs = [rows(r_dim, 0)]
    if first:
        out_shape.append(jax.ShapeDtypeStruct((m, r_dim), F32))
        out_specs.append(rows(r_dim, 0))

    scratch = [
        pltpu.VMEM((n_pairs, V7X_LANES, V7X_LANES), F32),
    ] + [pltpu.VMEM((tb, r_dim), BF16) for _ in range(5)] + [
        pltpu.VMEM((8, r_dim), F32),
        pltpu.VMEM((tb, r_dim), F32),
        pltpu.VMEM((n_prob, c2, 2 * c2), BF16),
        pltpu.VMEM((n_prob, c2, c2), F32),
        pltpu.VMEM((n_prob, c2, c2), BF16),
        pltpu.VMEM((n_prob, 2, CHUNK, c2), BF16),
        pltpu.VMEM((n_prob, c2, V7X_LANES), BF16),
        pltpu.VMEM((n_prob, c2, V7X_LANES), BF16),
        pltpu.VMEM((n_prob, c2, V7X_LANES), F32),
        pltpu.VMEM((n_pairs, CHUNK, V7X_LANES), BF16),
        pltpu.VMEM((n_pairs, CHUNK, V7X_LANES), F32),
    ]
    res = pl.pallas_call(
        functools.partial(_rwkv_kernel, first=first, n_pairs=n_pairs, head=head, d_lora=d_lora),
        grid=(batch, nt),
        in_specs=specs,
        out_specs=out_specs,
        out_shape=out_shape,
        scratch_shapes=scratch,
        compiler_params=pltpu.CompilerParams(
            dimension_semantics=("parallel", "arbitrary"), vmem_limit_bytes=V7X_VMEM_LIMIT_BYTES),
        name="rwkv_first" if first else "rwkv",
    )(*ops)
    return (res[0], res[1]) if first else (res[0], vfirst)


def _pool_mix(u, gp, carry, pos0, wp_ref, ps_ref):
    tb, p_dim = u.shape
    cg = p_dim // len(POOL_WINDOWS)
    ext = jnp.concatenate([carry, u], axis=0)
    sums = {}
    s = ext
    span = 1
    while span < MAX_POOL_WINDOW:
        s = s + pltpu.roll(s, span, 0)
        span *= 2
        sums[span] = s
    pos = (pos0 + 1 + lax.broadcasted_iota(jnp.int32, (tb, 1), 0)).astype(F32)
    outs = []
    for gi, win in enumerate(POOL_WINDOWS):
        cols = slice(gi * cg, (gi + 1) * cg)
        mean = sums[win][MAX_POOL_WINDOW:, cols] / jnp.minimum(pos, float(win))
        d = mean - u[:, cols]
        outs.append(_dot(d.astype(BF16), wp_ref[gi]))
    return jnp.concatenate(outs, axis=1) * ps_ref[...] * gp


def _out_kernel(x_ref, yr_ref, u_ref, gp_ref, p_ref, wo_ref, wp_ref, ps_ref, wpg_ref, bpg_ref, wple_ref, fg_ref, o_ref,
                carry_ref, *, last, blocks_per_seq):
    i = pl.program_id(0)
    tm = x_ref.shape[0]
    r_dim = yr_ref.shape[1]

    @pl.when(i == 0)
    def _():
        carry_ref[...] = jnp.zeros_like(carry_ref)

    keep = (i % blocks_per_seq != 0).astype(F32)
    x1 = x_ref[...] + _dot(yr_ref[...].astype(BF16), wo_ref[0:r_dim, :])
    yp = _pool_mix(u_ref[...], gp_ref[...], carry_ref[...] * keep,
                   (i % blocks_per_seq) * tm, wp_ref, ps_ref)
    carry_ref[...] = u_ref[tm - MAX_POOL_WINDOW:tm, :]
    x1 = x1 + _dot(yp.astype(BF16), wo_ref[r_dim:, :])
    gate = _sigmoid(_dot(x1.astype(BF16), wpg_ref[...]) + bpg_ref[...])
    x2 = x1 + gate * _dot(p_ref[...].astype(BF16), wple_ref[...])
    if last:
        ms = jnp.mean(x2 * x2, axis=-1, keepdims=True)
        x2 = x2 * lax.rsqrt(ms + RMS_EPS) * fg_ref[...]
    o_ref[...] = x2


def _out(x2, yr, z, p3, layer, w_out, w_pool, pool_scale, w_pg, b_pg, w_ple, final_g, *, tm, seq, last):
    m, d = x2.shape
    r_dim = yr.shape[1]
    pool_dim = pool_scale.shape[1]

    def whole(arr):
        return pl.BlockSpec(arr.shape, lambda i: (0,) * arr.ndim, pipeline_mode=pl.Buffered(1))

    return pl.pallas_call(
        functools.partial(_out_kernel, last=last, blocks_per_seq=seq // tm),
        grid=(m // tm,),
        in_specs=[
            pl.BlockSpec((tm, d), lambda i: (i, 0)),
            pl.BlockSpec((tm, r_dim), lambda i: (i, 0)),
            pl.BlockSpec((None, tm, pool_dim), lambda i: (4, i, 0)),
            pl.BlockSpec((None, tm, pool_dim), lambda i: (5, i, 0)),
            pl.BlockSpec((None, tm, p3.shape[2]), lambda i: (layer, i, 0)),
            whole(w_out), whole(w_pool), whole(pool_scale), whole(w_pg), whole(b_pg), whole(w_ple),
            whole(final_g),
        ],
        out_specs=pl.BlockSpec((tm, d), lambda i: (i, 0)),
        out_shape=jax.ShapeDtypeStruct((m, d), F32),
        scratch_shapes=[pltpu.VMEM((MAX_POOL_WINDOW, pool_dim), F32)],
        compiler_params=pltpu.CompilerParams(
            dimension_semantics=("arbitrary",), vmem_limit_bytes=V7X_VMEM_LIMIT_BYTES),
        name="out_proj",
    )(x2, yr, z, z, p3, w_out, w_pool, pool_scale, w_pg, b_pg, w_ple, final_g)


def _tile(n, cap):
    t = min(n, cap)
    assert n % t == 0
    return t


def kernel(x, p, norm_g, w_in, mu, w0, w_up, a0, a_up, v0, v_down, v_up, k_k, k_a, r_k, ln_w, ln_b, w_pool,
           pool_scale, w_out, w_ple, w_pg, b_pg, final_g):
    batch, seq, d = x.shape
    depth = w_in.shape[0]
    r_dim = w0.shape[1]
    heads, head = r_k.shape[1], r_k.shape[2]
    pool_dim = pool_scale.shape[1]
    n_shift = mu.shape[1]
    l_dim = n_shift - 3 * r_dim
    m = batch * seq
    assert 2 * head == V7X_LANES and heads % 2 == 0 and heads * head == r_dim and r_dim % V7X_MXU_DIM == 0
    assert l_dim % V7X_LANES == 0
    assert pool_dim == r_dim and pool_dim % len(POOL_WINDOWS) == 0

    tb_rwkv = _tile(seq, 256)
    tm_in = _tile(seq, 1024)
    tm_out = _tile(seq, 256)
    assert tb_rwkv % CHUNK == 0 and tb_rwkv // CHUNK <= 8 and tm_out >= MAX_POOL_WINDOW

    starts = [0, r_dim, 2 * r_dim, n_shift, n_shift + r_dim, n_shift + 2 * r_dim]
    w_in_p = jnp.stack([w_in[..., s0:s0 + r_dim] for s0 in starts], axis=1).astype(BF16)
    w_in_l = w_in[..., 3 * r_dim:n_shift].astype(BF16)
    w_out_b = w_out.astype(BF16)
    w_pg_b = w_pg.astype(BF16)
    w_ple_b = w_ple.astype(BF16)
    w_pool_b = w_pool.astype(BF16)
    p3 = p.reshape(depth, m, p.shape[-1])
    x2 = x.reshape(m, d)
    fg = final_g.reshape(1, d)

    vfirst = None
    for i in range(depth):
        z, z_l = _in_proj(x2, norm_g[i].reshape(1, d), w_in_p[i], w_in_l[i], mu[i, :3 * r_dim].reshape(1, -1),
                          mu[i, 3 * r_dim:].reshape(1, -1), tm=tm_in, tn=r_dim, seq=seq, n_shift_tiles=3,
                          gate_tiles=(3, 5))
        prm = {
            "w0": w0[i].reshape(1, -1), "w_up": w_up[i].astype(BF16),
            "a0": a0[i].reshape(1, -1), "a_up": a_up[i].astype(BF16),
            "k_k": k_k[i].reshape(1, -1), "k_a": k_a[i].reshape(1, -1), "r_k": r_k[i].reshape(1, -1),
            "ln_w": ln_w[i].reshape(1, -1), "ln_b": ln_b[i].reshape(1, -1),
        }
        if i > 0:
            prm.update({"v0": v0[i - 1].reshape(1, -1), "v_down": v_down[i - 1].astype(BF16),
                        "v_up": v_up[i - 1].astype(BF16)})
        yr, vfirst = _rwkv(z, z_l, vfirst, prm, batch=batch, seq=seq, tb=tb_rwkv, r_dim=r_dim,
                           heads=heads, head=head, first=(i == 0))
        x2 = _out(x2, yr, z, p3, i, w_out_b[i], w_pool_b[i], pool_scale[i].reshape(1, -1), w_pg_b[i],
                  b_pg[i].reshape(1, d), w_ple_b[i], fg, tm=tm_out, seq=seq, last=(i == depth - 1))
    return x2.reshape(batch, seq, d)
```

```python
import functools
import math

import jax
import jax.numpy as jnp
import numpy as np
from jax import lax
from jax.experimental import pallas as pl
from jax.experimental.pallas import tpu as pltpu

F32 = jnp.float32
BF16 = jnp.bfloat16

RMS_EPS = 1e-6
GN_EPS = 64e-5
KK_EPS = 1e-12
POOL_WINDOWS = (2, 4, 8, 16)

V7X_LANES = 128
V7X_MXU_DIM = 256
V7X_VMEM_LIMIT_BYTES = 56 * 2**20
CHUNK = 64
SWEEP_GROUP = 2
MAX_POOL_WINDOW = max(POOL_WINDOWS)


def _dot(a, b):
    return jnp.dot(a, b, preferred_element_type=F32)


def _dot_nt(a, b):
    return lax.dot_general(a, b, (((1,), (1,)), ((), ())), preferred_element_type=F32)


def _dot_tn(a, b):
    return lax.dot_general(a, b, (((0,), (0,)), ((), ())), preferred_element_type=F32)


def _sigmoid(x):
    return 0.5 * jnp.tanh(0.5 * x) + 0.5


def _split_bf16(x, parts):
    out = []
    for _ in range(parts):
        hi = x.astype(BF16)
        out.append(hi)
        x = x - hi.astype(F32)
    return out


def _dot_0_1(m01, x, parts):
    acc = None
    for part in _split_bf16(x, parts):
        d = _dot(m01, part)
        acc = d if acc is None else acc + d
    return acc


def _seg_sum(x, seg):
    w = seg.shape[0]
    xb = x.astype(BF16)
    return jnp.concatenate(
        [_dot(xb[:, q * w:(q + 1) * w], seg) for q in range(x.shape[1] // w)], axis=1)


def _in_proj_kernel(x_ref, g_ref, w_ref, wl_ref, mu_ref, mul_ref, o_ref, ol_ref, h_scr, carry_ref, carryl_ref, *,
                    n_shift_tiles, gate_tiles, blocks_per_seq, row_chunk, normalized):
    i = pl.program_id(0)
    j = pl.program_id(1)
    tm = x_ref.shape[0]
    n_rc = tm // row_chunk
    keep = (i % blocks_per_seq != 0).astype(F32)
    row = lax.broadcasted_iota(jnp.int32, (row_chunk, 1), 0)

    def rows(rc):
        return slice(rc * row_chunk, (rc + 1) * row_chunk)

    def shift_mix(zv, prev_row, mu):
        prev = jnp.where(row == 0, prev_row, pltpu.roll(zv, 1, 0))
        return zv + (prev - zv) * mu

    def normalize(rc):
        x = x_ref[rows(rc), :]
        ms = jnp.mean(x * x, axis=-1, keepdims=True)
        h = (x * lax.rsqrt(ms + RMS_EPS) * g_ref[...]).astype(BF16)
        h_scr[rows(rc), :] = h
        return h

    def hidden(rc, first_tile):
        if normalized:
            return x_ref[rows(rc), :]
        return normalize(rc) if first_tile else h_scr[rows(rc), :]

    def shifted_tile(with_norm):
        prev = carry_ref[j, 0:1, :] * keep
        prev_l = carryl_ref[0:1, :] * keep
        for rc in range(n_rc):
            h = hidden(rc, with_norm)
            z = _dot(h, w_ref[...])
            o_ref[rows(rc), :] = shift_mix(z, prev, mu_ref[...])
            prev = z[row_chunk - 1:row_chunk]
            if with_norm:
                zl = _dot(h, wl_ref[...])
                ol_ref[rows(rc), :] = shift_mix(zl, prev_l, mul_ref[...])
                prev_l = zl[row_chunk - 1:row_chunk]
        carry_ref[j, 0:1, :] = prev
        if with_norm:
            carryl_ref[0:1, :] = prev_l

    @pl.when((i == 0) & (j == 0))
    def _():
        carry_ref[...] = jnp.zeros_like(carry_ref)
        carryl_ref[...] = jnp.zeros_like(carryl_ref)

    is_gate = functools.reduce(lambda a, b: a | b, [j == t for t in gate_tiles])

    @pl.when(j == 0)
    def _():
        shifted_tile(True)

    @pl.when((j > 0) & (j < n_shift_tiles))
    def _():
        shifted_tile(False)

    @pl.when(is_gate)
    def _():
        for rc in range(n_rc):
            z = _dot(hidden(rc, False), w_ref[...])
            o_ref[rows(rc), :] = z * _sigmoid(z)

    @pl.when((j >= n_shift_tiles) & jnp.logical_not(is_gate))
    def _():
        for rc in range(n_rc):
            o_ref[rows(rc), :] = _dot(hidden(rc, False), w_ref[...])


def _in_proj(x2, g, w, w_l, mu_main, mu_l, *, tm, tn, seq, n_shift_tiles, gate_tiles, normalized):
    m, d = x2.shape
    n = w.shape[1]
    n_l = w_l.shape[1]
    n_tiles = n // tn
    return pl.pallas_call(
        functools.partial(_in_proj_kernel, n_shift_tiles=n_shift_tiles, gate_tiles=gate_tiles,
                          blocks_per_seq=seq // tm, row_chunk=min(tm, V7X_MXU_DIM), normalized=normalized),
        grid=(m // tm, n_tiles),
        in_specs=[
            pl.BlockSpec((tm, d), lambda i, j: (i, 0)),
            pl.BlockSpec((1, d), lambda i, j: (0, 0)),
            pl.BlockSpec((d, tn), lambda i, j: (0, j)),
            pl.BlockSpec((d, n_l), lambda i, j: (0, 0)),
            pl.BlockSpec((1, tn), lambda i, j: (0, jnp.minimum(j, n_shift_tiles - 1))),
            pl.BlockSpec((1, n_l), lambda i, j: (0, 0)),
        ],
        out_specs=[pl.BlockSpec((tm, tn), lambda i, j: (i, j)),
                   pl.BlockSpec((tm, n_l), lambda i, j: (i, 0))],
        out_shape=[jax.ShapeDtypeStruct((m, n), F32), jax.ShapeDtypeStruct((m, n_l), F32)],
        scratch_shapes=[pltpu.VMEM((8, d) if normalized else (tm, d), BF16),
                        pltpu.VMEM((n_shift_tiles, 8, tn), F32),
                        pltpu.VMEM((8, n_l), F32)],
        compiler_params=pltpu.CompilerParams(
            dimension_semantics=("arbitrary", "arbitrary"), vmem_limit_bytes=V7X_VMEM_LIMIT_BYTES),
        name="in_proj",
    )(x2, g, w, w_l, mu_main, mu_l)


def _rwkv_kernel(*refs, first, n_pairs, head, d_lora):
    it = iter(refs)
    z_ref, zl_ref = next(it), next(it)
    vf_ref = None if first else next(it)
    w0_ref, wup_ref, a0_ref, aup_ref = (next(it) for _ in range(4))
    if not first:
        v0_ref, vdn_ref, vup_ref = next(it), next(it), next(it)
    kk_ref, ka_ref, rk_ref, lnw_ref, lnb_ref, seg_ref, segn_ref, tri_ref = (next(it) for _ in range(8))
    y_ref = next(it)
    vfo_ref = next(it) if first else None
    (state_ref, at_s, rt_s, bt_s, kt_s, v_s, wc_s, y_s,
     ch_s, tinv_s, aak_s, ar_s, xloc_s, ta_s, uloc_s, u_s, yr_s) = it

    tb = z_ref.shape[0]
    r_dim = w0_ref.shape[1]
    c = CHUNK
    n_chunks = tb // c
    levels = int(math.log2(c))

    @pl.when(pl.program_id(1) == 0)
    def _():
        state_ref[...] = jnp.zeros_like(state_ref)

    r = z_ref[:, 0:r_dim]
    k = z_ref[:, r_dim:2 * r_dim]
    v = z_ref[:, 2 * r_dim:3 * r_dim]
    lora_in = zl_ref[...]

    wd = lora_in[:, 0:d_lora]
    ad = lora_in[:, d_lora:]
    logw = -math.exp(-0.5) * _sigmoid(w0_ref[...] + _dot(jnp.tanh(wd).astype(BF16), wup_ref[...]))
    a = _sigmoid(a0_ref[...] + _dot(ad.astype(BF16), aup_ref[...]))

    if first:
        vfo_ref[...] = v
    else:
        low = _dot(v.astype(BF16), vdn_ref[...])
        nu = _sigmoid(v0_ref[...] + _dot(low.astype(BF16), vup_ref[...]))
        v = v + (vf_ref[...] - v) * nu

    seg = seg_ref[...]
    kk = k * kk_ref[...]
    kk = kk * lax.rsqrt(jnp.maximum(_seg_sum(kk * kk, seg), KK_EPS * KK_EPS))
    k = k * (1.0 + (a - 1.0) * ka_ref[...])
    b = kk * a

    cs = _dot_0_1(tri_ref[...], logw, 2)
    w_inv = jnp.exp(-cs)
    at_s[...] = (-kk * jnp.exp(cs - logw)).astype(BF16)
    rt_s[...] = (r * jnp.exp(cs)).astype(BF16)
    bt_s[...] = (b * w_inv).astype(BF16)
    kt_s[...] = (k * w_inv).astype(BF16)
    v_s[...] = v.astype(BF16)
    for i in range(n_chunks):
        wc_s[i:i + 1, :] = jnp.exp(cs[(i + 1) * c - 1:(i + 1) * c])

    bonus = _seg_sum(r * k * rk_ref[...], seg) * v
    gate = z_ref[:, 3 * r_dim:4 * r_dim]

    lane = lax.broadcasted_iota(jnp.int32, (c, V7X_LANES), 1)
    t_row = lax.broadcasted_iota(jnp.int32, (c, V7X_LANES), 0)
    s_col = lane % head
    lo_half = lane < head
    hi_half = lane >= head
    strict_lo = lo_half & (s_col < t_row)
    strict_hi = hi_half & (s_col < t_row)
    incl = s_col <= t_row
    eye = (lax.broadcasted_iota(jnp.int32, (2 * c, 2 * c), 0)
           == lax.broadcasted_iota(jnp.int32, (2 * c, 2 * c), 1)).astype(F32)
    same_head = (lax.broadcasted_iota(jnp.int32, (V7X_LANES, V7X_LANES), 0) // head
                 == lax.broadcasted_iota(jnp.int32, (V7X_LANES, V7X_LANES), 1) // head)

    def tile(src, ci, p):
        return src[ci * c:(ci + 1) * c, p * V7X_LANES:(p + 1) * V7X_LANES]

    def head_rows(xv):
        zero = jnp.zeros_like(xv)
        return jnp.concatenate([jnp.where(lo_half, xv, zero), jnp.where(lo_half, zero, xv)], axis=0)

    lw = V7X_LANES

    def gram(q, ci, p):
        bt, kt = tile(bt_s, ci, p), tile(kt_s, ci, p)
        zero = jnp.zeros_like(bt)
        rhs = jnp.concatenate([jnp.where(lo_half, bt, zero), jnp.where(lo_half, kt, zero),
                               jnp.where(lo_half, zero, kt), jnp.where(lo_half, zero, bt)], axis=0)
        lhs = jnp.concatenate([tile(at_s, ci, p), tile(rt_s, ci, p)], axis=0)
        gm = _dot_nt(lhs, rhs)
        g0, g1 = gm[0:c, 0:lw], gm[0:c, lw:2 * lw]
        nmat = jnp.concatenate([jnp.where(strict_lo, g0, 0.0), jnp.where(strict_hi, g1, 0.0)], axis=0)
        aak_s[q] = jnp.concatenate([jnp.where(strict_hi, g0, 0.0), jnp.where(strict_lo, g1, 0.0)],
                                   axis=0).astype(BF16)
        ar_s[q, 0] = jnp.where(incl, gm[c:2 * c, 0:lw], 0.0).astype(BF16)
        ar_s[q, 1] = jnp.where(incl, gm[c:2 * c, lw:2 * lw], 0.0).astype(BF16)
        tinv = eye + nmat
        tinv_s[q] = tinv
        ch_s[q, :, 0:lw] = nmat.astype(BF16)
        ch_s[q, :, lw:2 * lw] = tinv.astype(BF16)

    def square(q):
        pw = ch_s[q, :, 0:lw]
        ch_s[q, :, 0:lw] = _dot(pw, pw).astype(BF16)

    def square_and_extend(q):
        res = _dot(ch_s[q, :, 0:lw], ch_s[q])
        tinv = tinv_s[q] + res[:, lw:2 * lw]
        tinv_s[q] = tinv
        ch_s[q, :, 0:lw] = res[:, 0:lw].astype(BF16)
        ch_s[q, :, lw:2 * lw] = tinv.astype(BF16)

    def extend_last(q):
        tinv = tinv_s[q] + _dot(ch_s[q, :, 0:lw], ch_s[q, :, lw:2 * lw])
        ch_s[q, :, lw:2 * lw] = tinv.astype(BF16)

    def local_x(q, ci, p):
        vt = tile(v_s, ci, p)
        xloc_s[q] = _dot(aak_s[q], jnp.concatenate([vt, vt], axis=0)).astype(BF16)

    def apply_inverse(q, ci, p):
        tu = _dot(ch_s[q, :, lw:2 * lw], jnp.concatenate([head_rows(tile(at_s, ci, p)), xloc_s[q]], axis=1))
        ta_s[q] = tu[:, 0:lw].astype(BF16)
        uloc_s[q] = tu[:, lw:2 * lw]

    def sweep_stages(chunks):
        qs = [(ci * n_pairs + p, ci, p) for ci in chunks for p in range(n_pairs)]
        stages = [[functools.partial(gram, q, ci, p) for q, ci, p in qs],
                  [functools.partial(square, q) for q, _, _ in qs]]
        stages += [[functools.partial(square_and_extend, q) for q, _, _ in qs] for _ in range(levels - 2)]
        stages += [[functools.partial(extend_last, q) for q, _, _ in qs],
                   [functools.partial(local_x, q, ci, p) for q, ci, p in qs],
                   [functools.partial(apply_inverse, q, ci, p) for q, ci, p in qs]]
        return stages

    def read_state(ci):
        for p in range(n_pairs):
            q = ci * n_pairs + p
            lhs = jnp.concatenate([ta_s[q], tile(rt_s, ci, p)], axis=0)
            xh = _dot_nt(lhs, state_ref[p].astype(BF16))
            u2 = xh[0:2 * c] + uloc_s[q]
            u_s[p] = jnp.where(lo_half, u2[0:c], u2[c:2 * c]).astype(BF16)
            yr_s[p] = xh[2 * c:3 * c]

    def write_state(ci):
        for p in range(n_pairs):
            q = ci * n_pairs + p
            ls = slice(p * lw, (p + 1) * lw)
            ut, vt = u_s[p], tile(v_s, ci, p)
            uv = jnp.concatenate([ut, vt], axis=0)
            upd = _dot_tn(uv, jnp.concatenate([tile(bt_s, ci, p), tile(kt_s, ci, p)], axis=0))
            state_ref[p] = (state_ref[p] + jnp.where(same_head, upd, 0.0)) * wc_s[ci:ci + 1, ls]
            y0 = _dot(ar_s[q, 0], uv)
            y1 = _dot(ar_s[q, 1], jnp.concatenate([vt, ut], axis=0))
            y_s[ci * c:(ci + 1) * c, ls] = yr_s[p] + jnp.where(lo_half, y0, y1)

    groups = [list(range(g0, min(g0 + SWEEP_GROUP, n_chunks))) for g0 in range(0, n_chunks, SWEEP_GROUP)]
    prev = []
    for grp in groups:
        stages = sweep_stages(grp)
        riders = [functools.partial(f, ci) for ci in prev for f in (read_state, write_state)]
        done = 0
        for si, stage in enumerate(stages):
            for task in stage:
                task()
            want = (si + 1) * len(riders) // len(stages)
            for rider in riders[done:want]:
                rider()
            done = want
        prev = grp
    for ci in prev:
        read_state(ci)
        write_state(ci)

    y = y_s[...]
    segn = segn_ref[...]
    yc = y - _seg_sum(y, segn)
    yn = yc * lax.rsqrt(_seg_sum(yc * yc, segn) + GN_EPS) * lnw_ref[...] + lnb_ref[...]
    y_ref[...] = (yn + bonus) * gate


def _rwkv_constants(tb, head):
    c = CHUNK
    i = np.arange(tb)
    tri = ((i[:, None] // c) == (i[None, :] // c)) & (i[None, :] <= i[:, None])
    lanes = np.arange(V7X_MXU_DIM) // head
    seg = (lanes[:, None] == lanes[None, :]).astype(np.float32)
    return jnp.asarray(seg, BF16), jnp.asarray(seg / head, BF16), jnp.asarray(tri, BF16)


def _rwkv(z, z_l, vfirst, prm, *, batch, seq, tb, r_dim, heads, head, first):
    m = batch * seq
    nt = seq // tb
    d_lora = prm["w_up"].shape[0]
    l_dim = d_lora + prm["a_up"].shape[0]
    n_pairs = heads // 2
    seg, segn, tri = _rwkv_constants(tb, head)
    n_prob = (tb // CHUNK) * n_pairs
    c2 = 2 * CHUNK

    def rows(width, col):
        return pl.BlockSpec((tb, width), lambda bi, ti: (bi * nt + ti, col))

    def whole(arr):
        return pl.BlockSpec(arr.shape, lambda bi, ti: (0,) * arr.ndim, pipeline_mode=pl.Buffered(1))

    ops = [z, z_l]
    specs = [rows(4 * r_dim, 0), rows(l_dim, 0)]
    if not first:
        ops.append(vfirst)
        specs.append(rows(r_dim, 0))
    names = ["w0", "w_up", "a0", "a_up"]
    if not first:
        names += ["v0", "v_down", "v_up"]
    names += ["k_k", "k_a", "r_k", "ln_w", "ln_b"]
    consts = [prm[nm] for nm in names] + [seg, segn, tri]
    ops += consts
    specs += [whole(a) for a in consts]

    out_shape = [jax.ShapeDtypeStruct((m, r_dim), F32)]
    out_specs = [rows(r_dim, 0)]
    if first:
        out_shape.append(jax.ShapeDtypeStruct((m, r_dim), F32))
        out_specs.append(rows(r_dim, 0))

    scratch = [
        pltpu.VMEM((n_pairs, V7X_LANES, V7X_LANES), F32),
    ] + [pltpu.VMEM((tb, r_dim), BF16) for _ in range(5)] + [
        pltpu.VMEM((8, r_dim), F32),
        pltpu.VMEM((tb, r_dim), F32),
        pltpu.VMEM((n_prob, c2, 2 * c2), BF16),
        pltpu.VMEM((n_prob, c2, c2), F32),
        pltpu.VMEM((n_prob, c2, c2), BF16),
        pltpu.VMEM((n_prob, 2, CHUNK, c2), BF16),
        pltpu.VMEM((n_prob, c2, V7X_LANES), BF16),
        pltpu.VMEM((n_prob, c2, V7X_LANES), BF16),
        pltpu.VMEM((n_prob, c2, V7X_LANES), F32),
        pltpu.VMEM((n_pairs, CHUNK, V7X_LANES), BF16),
        pltpu.VMEM((n_pairs, CHUNK, V7X_LANES), F32),
    ]
    res = pl.pallas_call(
        functools.partial(_rwkv_kernel, first=first, n_pairs=n_pairs, head=head, d_lora=d_lora),
        grid=(batch, nt),
        in_specs=specs,
        out_specs=out_specs,
        out_shape=out_shape,
        scratch_shapes=scratch,
        compiler_params=pltpu.CompilerParams(
            dimension_semantics=("parallel", "arbitrary"), vmem_limit_bytes=V7X_VMEM_LIMIT_BYTES),
        name="rwkv_first" if first else "rwkv",
    )(*ops)
    return (res[0], res[1]) if first else (res[0], vfirst)


def _pool_mix(u, gp, carry, pos0, wp_ref, ps_ref):
    tb, p_dim = u.shape
    cg = p_dim // len(POOL_WINDOWS)
    ext = jnp.concatenate([carry, u], axis=0)
    sums = {}
    s = ext
    span = 1
    while span < MAX_POOL_WINDOW:
        s = s + pltpu.roll(s, span, 0)
        span *= 2
        sums[span] = s
    pos = (pos0 + 1 + lax.broadcasted_iota(jnp.int32, (tb, 1), 0)).astype(F32)
    outs = []
    for gi, win in enumerate(POOL_WINDOWS):
        cols = slice(gi * cg, (gi + 1) * cg)
        mean = sums[win][MAX_POOL_WINDOW:, cols] / jnp.minimum(pos, float(win))
        d = mean - u[:, cols]
        outs.append(_dot(d.astype(BF16), wp_ref[gi]))
    return jnp.concatenate(outs, axis=1) * ps_ref[...] * gp


def _out_kernel(x_ref, yr_ref, zp_ref, p_ref, wo_ref, wp_ref, ps_ref, wpg_ref, bpg_ref, wple_ref, ng_ref, *rest,
                last, blocks_per_seq):
    if last:
        o_ref, carry_ref = rest
    else:
        o_ref, h_ref, carry_ref = rest
    i = pl.program_id(0)
    tm = x_ref.shape[0]
    r_dim = yr_ref.shape[1]
    p_dim = ps_ref.shape[1]

    @pl.when(i == 0)
    def _():
        carry_ref[...] = jnp.zeros_like(carry_ref)

    keep = (i % blocks_per_seq != 0).astype(F32)
    x1 = x_ref[...] + _dot(yr_ref[...].astype(BF16), wo_ref[0:r_dim, :])
    yp = _pool_mix(zp_ref[:, 0:p_dim], zp_ref[:, p_dim:2 * p_dim], carry_ref[...] * keep,
                   (i % blocks_per_seq) * tm, wp_ref, ps_ref)
    carry_ref[...] = zp_ref[tm - MAX_POOL_WINDOW:tm, 0:p_dim]
    x1 = x1 + _dot(yp.astype(BF16), wo_ref[r_dim:, :])
    gate = _sigmoid(_dot(x1.astype(BF16), wpg_ref[...]) + bpg_ref[...])
    x2 = x1 + gate * _dot(p_ref[...].astype(BF16), wple_ref[...])
    ms = jnp.mean(x2 * x2, axis=-1, keepdims=True)
    normed = x2 * lax.rsqrt(ms + RMS_EPS) * ng_ref[...]
    if last:
        o_ref[...] = normed
    else:
        o_ref[...] = x2
        h_ref[...] = normed.astype(BF16)


def _out(x2, yr, z, p3, layer, w_out, w_pool, pool_scale, w_pg, b_pg, w_ple, next_g, *, tm, seq, last):
    m, d = x2.shape
    r_dim = yr.shape[1]
    pool_dim = pool_scale.shape[1]

    def whole(arr):
        return pl.BlockSpec(arr.shape, lambda i: (0,) * arr.ndim, pipeline_mode=pl.Buffered(1))

    return pl.pallas_call(
        functools.partial(_out_kernel, last=last, blocks_per_seq=seq // tm),
        grid=(m // tm,),
        in_specs=[
            pl.BlockSpec((tm, d), lambda i: (i, 0)),
            pl.BlockSpec((tm, r_dim), lambda i: (i, 0)),
            pl.BlockSpec((tm, 2 * pool_dim), lambda i: (i, (4 * r_dim) // (2 * pool_dim))),
            pl.BlockSpec((None, tm, p3.shape[2]), lambda i: (layer, i, 0)),
            whole(w_out), whole(w_pool), whole(pool_scale), whole(w_pg), whole(b_pg), whole(w_ple),
            whole(next_g),
        ],
        out_specs=[pl.BlockSpec((tm, d), lambda i: (i, 0))] * (1 if last else 2),
        out_shape=[jax.ShapeDtypeStruct((m, d), F32)] + ([] if last else [jax.ShapeDtypeStruct((m, d), BF16)]),
        scratch_shapes=[pltpu.VMEM((MAX_POOL_WINDOW, pool_dim), F32)],
        compiler_params=pltpu.CompilerParams(
            dimension_semantics=("arbitrary",), vmem_limit_bytes=V7X_VMEM_LIMIT_BYTES),
        name="out_proj",
    )(x2, yr, z, p3, w_out, w_pool, pool_scale, w_pg, b_pg, w_ple, next_g)


def _tile(n, cap):
    t = min(n, cap)
    assert n % t == 0
    return t


def kernel(x, p, norm_g, w_in, mu, w0, w_up, a0, a_up, v0, v_down, v_up, k_k, k_a, r_k, ln_w, ln_b, w_pool,
           pool_scale, w_out, w_ple, w_pg, b_pg, final_g):
    batch, seq, d = x.shape
    depth = w_in.shape[0]
    r_dim = w0.shape[1]
    heads, head = r_k.shape[1], r_k.shape[2]
    pool_dim = pool_scale.shape[1]
    n_shift = mu.shape[1]
    l_dim = n_shift - 3 * r_dim
    m = batch * seq
    assert 2 * head == V7X_LANES and heads % 2 == 0 and heads * head == r_dim and r_dim % V7X_MXU_DIM == 0
    assert l_dim % V7X_LANES == 0
    assert pool_dim == r_dim and pool_dim % len(POOL_WINDOWS) == 0

    tb_rwkv = _tile(seq, 256)
    tm_in = _tile(seq, 1024)
    tm_in_h = _tile(seq, 2048)
    tm_out = _tile(seq, 256)
    assert tb_rwkv % CHUNK == 0 and tb_rwkv // CHUNK <= 8 and tm_out >= MAX_POOL_WINDOW

    w_in_p = jnp.concatenate(
        [w_in[..., :3 * r_dim], w_in[..., n_shift:n_shift + r_dim], w_in[..., n_shift + r_dim:]],
        axis=-1).astype(BF16)
    w_in_l = w_in[..., 3 * r_dim:n_shift].astype(BF16)
    w_out_b = w_out.astype(BF16)
    w_pg_b = w_pg.astype(BF16)
    w_ple_b = w_ple.astype(BF16)
    w_pool_b = w_pool.astype(BF16)
    p3 = p.reshape(depth, m, p.shape[-1])
    x2 = x.reshape(m, d)
    fg = final_g.reshape(1, d)

    vfirst = None
    for i in range(depth):
        z, z_l = _in_proj(x2 if i == 0 else h, norm_g[i].reshape(1, d), w_in_p[i], w_in_l[i],
                          mu[i, :3 * r_dim].reshape(1, -1), mu[i, 3 * r_dim:].reshape(1, -1),
                          tm=tm_in if i == 0 else tm_in_h, tn=r_dim, seq=seq, n_shift_tiles=3, gate_tiles=(3, 5),
                          normalized=(i > 0))
        prm = {
            "w0": w0[i].reshape(1, -1), "w_up": w_up[i].astype(BF16),
            "a0": a0[i].reshape(1, -1), "a_up": a_up[i].astype(BF16),
            "k_k": k_k[i].reshape(1, -1), "k_a": k_a[i].reshape(1, -1), "r_k": r_k[i].reshape(1, -1),
            "ln_w": ln_w[i].reshape(1, -1), "ln_b": ln_b[i].reshape(1, -1),
        }
        if i > 0:
            prm.update({"v0": v0[i - 1].reshape(1, -1), "v_down": v_down[i - 1].astype(BF16),
                        "v_up": v_up[i - 1].astype(BF16)})
        yr, vfirst = _rwkv(z, z_l, vfirst, prm, batch=batch, seq=seq, tb=tb_rwkv, r_dim=r_dim,
                           heads=heads, head=head, first=(i == 0))
        last = i == depth - 1
        res = _out(x2, yr, z, p3, i, w_out_b[i], w_pool_b[i], pool_scale[i].reshape(1, -1), w_pg_b[i],
                   b_pg[i].reshape(1, d), w_ple_b[i], fg if last else norm_g[i + 1].reshape(1, d), tm=tm_out,
                   seq=seq, last=last)
        x2, h = (res[0], None) if last else res
    return x2.reshape(batch, seq, d)
```

```python
import functools
import math

import jax
import jax.numpy as jnp
import numpy as np
from jax import lax
from jax.experimental import pallas as pl
from jax.experimental.pallas import tpu as pltpu

F32 = jnp.float32
BF16 = jnp.bfloat16

RMS_EPS = 1e-6
GN_EPS = 64e-5
KK_EPS = 1e-12
POOL_WINDOWS = (2, 4, 8, 16)

V7X_LANES = 128
V7X_MXU_DIM = 256
V7X_VMEM_LIMIT_BYTES = 56 * 2**20
CHUNK = 64
SWEEP_GROUP = 2
PREP_TILE_LANES = 256
MAX_POOL_WINDOW = max(POOL_WINDOWS)


def _dot(a, b):
    return jnp.dot(a, b, preferred_element_type=F32)


def _dot_nt(a, b):
    return lax.dot_general(a, b, (((1,), (1,)), ((), ())), preferred_element_type=F32)


def _dot_tn(a, b):
    return lax.dot_general(a, b, (((0,), (0,)), ((), ())), preferred_element_type=F32)


def _sigmoid(x):
    return 0.5 * jnp.tanh(0.5 * x) + 0.5


def _split_bf16(x, parts):
    out = []
    for _ in range(parts):
        hi = x.astype(BF16)
        out.append(hi)
        x = x - hi.astype(F32)
    return out


def _dot_0_1(m01, x, parts):
    acc = None
    for part in _split_bf16(x, parts):
        d = _dot(m01, part)
        acc = d if acc is None else acc + d
    return acc


def _seg_sum(x, seg):
    w = seg.shape[0]
    xb = x.astype(BF16)
    return jnp.concatenate(
        [_dot(xb[:, q * w:(q + 1) * w], seg) for q in range(x.shape[1] // w)], axis=1)


def _in_proj_kernel(x_ref, g_ref, w_ref, wl_ref, mu_ref, mul_ref, o_ref, ol_ref, h_scr, carry_ref, carryl_ref, *,
                    n_shift_tiles, gate_tiles, blocks_per_seq, row_chunk, normalized):
    i = pl.program_id(0)
    j = pl.program_id(1)
    tm = x_ref.shape[0]
    n_rc = tm // row_chunk
    keep = (i % blocks_per_seq != 0).astype(F32)
    row = lax.broadcasted_iota(jnp.int32, (row_chunk, 1), 0)

    def rows(rc):
        return slice(rc * row_chunk, (rc + 1) * row_chunk)

    def shift_mix(zv, prev_row, mu):
        prev = jnp.where(row == 0, prev_row, pltpu.roll(zv, 1, 0))
        return zv + (prev - zv) * mu

    def normalize(rc):
        x = x_ref[rows(rc), :]
        ms = jnp.mean(x * x, axis=-1, keepdims=True)
        h = (x * lax.rsqrt(ms + RMS_EPS) * g_ref[...]).astype(BF16)
        h_scr[rows(rc), :] = h
        return h

    def hidden(rc, first_tile):
        if normalized:
            return x_ref[rows(rc), :]
        return normalize(rc) if first_tile else h_scr[rows(rc), :]

    def shifted_tile(with_norm):
        prev = carry_ref[j, 0:1, :] * keep
        prev_l = carryl_ref[0:1, :] * keep
        for rc in range(n_rc):
            h = hidden(rc, with_norm)
            z = _dot(h, w_ref[...])
            o_ref[rows(rc), :] = shift_mix(z, prev, mu_ref[...])
            prev = z[row_chunk - 1:row_chunk]
            if with_norm:
                zl = _dot(h, wl_ref[...])
                ol_ref[rows(rc), :] = shift_mix(zl, prev_l, mul_ref[...])
                prev_l = zl[row_chunk - 1:row_chunk]
        carry_ref[j, 0:1, :] = prev
        if with_norm:
            carryl_ref[0:1, :] = prev_l

    @pl.when((i == 0) & (j == 0))
    def _():
        carry_ref[...] = jnp.zeros_like(carry_ref)
        carryl_ref[...] = jnp.zeros_like(carryl_ref)

    is_gate = functools.reduce(lambda a, b: a | b, [j == t for t in gate_tiles])

    @pl.when(j == 0)
    def _():
        shifted_tile(True)

    @pl.when((j > 0) & (j < n_shift_tiles))
    def _():
        shifted_tile(False)

    @pl.when(is_gate)
    def _():
        for rc in range(n_rc):
            z = _dot(hidden(rc, False), w_ref[...])
            o_ref[rows(rc), :] = z * _sigmoid(z)

    @pl.when((j >= n_shift_tiles) & jnp.logical_not(is_gate))
    def _():
        for rc in range(n_rc):
            o_ref[rows(rc), :] = _dot(hidden(rc, False), w_ref[...])


def _in_proj(x2, g, w, w_l, mu_main, mu_l, *, tm, tn, seq, n_shift_tiles, gate_tiles, normalized):
    m, d = x2.shape
    n = w.shape[1]
    n_l = w_l.shape[1]
    n_tiles = n // tn
    return pl.pallas_call(
        functools.partial(_in_proj_kernel, n_shift_tiles=n_shift_tiles, gate_tiles=gate_tiles,
                          blocks_per_seq=seq // tm, row_chunk=min(tm, V7X_MXU_DIM), normalized=normalized),
        grid=(m // tm, n_tiles),
        in_specs=[
            pl.BlockSpec((tm, d), lambda i, j: (i, 0)),
            pl.BlockSpec((1, d), lambda i, j: (0, 0)),
            pl.BlockSpec((d, tn), lambda i, j: (0, j)),
            pl.BlockSpec((d, n_l), lambda i, j: (0, 0)),
            pl.BlockSpec((1, tn), lambda i, j: (0, jnp.minimum(j, n_shift_tiles - 1))),
            pl.BlockSpec((1, n_l), lambda i, j: (0, 0)),
        ],
        out_specs=[pl.BlockSpec((tm, tn), lambda i, j: (i, j)),
                   pl.BlockSpec((tm, n_l), lambda i, j: (i, 0))],
        out_shape=[jax.ShapeDtypeStruct((m, n), F32), jax.ShapeDtypeStruct((m, n_l), F32)],
        scratch_shapes=[pltpu.VMEM((8, d) if normalized else (tm, d), BF16),
                        pltpu.VMEM((n_shift_tiles, 8, tn), F32),
                        pltpu.VMEM((8, n_l), F32)],
        compiler_params=pltpu.CompilerParams(
            dimension_semantics=("arbitrary", "arbitrary"), vmem_limit_bytes=V7X_VMEM_LIMIT_BYTES),
        name="in_proj",
    )(x2, g, w, w_l, mu_main, mu_l)


def _rwkv_kernel(*refs, first, n_pairs, head, d_lora):
    it = iter(refs)
    z_ref, zl_ref = next(it), next(it)
    vf_ref = None if first else next(it)
    w0_ref, wup_ref, a0_ref, aup_ref = (next(it) for _ in range(4))
    if not first:
        v0_ref, vdn_ref, vup_ref = next(it), next(it), next(it)
    kk_ref, ka_ref, rk_ref, lnw_ref, lnb_ref, seg_ref, segn_ref, tri_ref = (next(it) for _ in range(8))
    y_ref = next(it)
    vfo_ref = next(it) if first else None
    (state_ref, at_s, rt_s, bt_s, kt_s, v_s, wc_s, y_s, bonus_s,
     ch_s, tinv_s, aak_s, ar_s, xloc_s, ta_s, uloc_s, u_s, yr_s) = it

    tb = z_ref.shape[0]
    r_dim = w0_ref.shape[1]
    c = CHUNK
    n_chunks = tb // c
    levels = int(math.log2(c))

    @pl.when(pl.program_id(1) == 0)
    def _():
        state_ref[...] = jnp.zeros_like(state_ref)

    gw = PREP_TILE_LANES
    n_groups = r_dim // gw
    seg = seg_ref[...]
    lora_in = zl_ref[...]
    dw = _dot(jnp.tanh(lora_in[:, 0:d_lora]).astype(BF16), wup_ref[...])
    da = _dot(lora_in[:, d_lora:].astype(BF16), aup_ref[...])
    kk0 = z_ref[:, r_dim:2 * r_dim] * kk_ref[...]
    ss = _seg_sum(kk0 * kk0, seg)
    if first:
        vfo_ref[...] = z_ref[:, 2 * r_dim:3 * r_dim]
        nu_pre = None
    else:
        low = _dot(z_ref[:, 2 * r_dim:3 * r_dim].astype(BF16), vdn_ref[...])
        nu_pre = _dot(low.astype(BF16), vup_ref[...])

    def assemble(tiles):
        return jnp.concatenate(
            [jnp.concatenate([tiles[ci][g] for g in range(n_groups)], axis=1) for ci in range(n_chunks)], axis=0)

    half_decay = -0.5 * math.exp(-0.5)
    ka_half = 0.5 * ka_ref[...]
    ka_rest = 1.0 - ka_half
    keep = [[None] * n_groups for _ in range(n_chunks)]
    hi_t = [[None] * n_groups for _ in range(n_chunks)]
    lo_t = [[None] * n_groups for _ in range(n_chunks)]
    rkr_t = [[None] * n_groups for _ in range(n_chunks)]
    for ci in range(n_chunks):
        for g in range(n_groups):
            rows = slice(ci * c, (ci + 1) * c)
            cols = slice(g * gw, (g + 1) * gw)
            logw = half_decay * jnp.tanh(0.5 * (w0_ref[:, cols] + dw[rows, cols])) + half_decay
            ta = jnp.tanh(0.5 * (a0_ref[:, cols] + da[rows, cols]))
            v = z_ref[rows, 2 * r_dim + g * gw:2 * r_dim + (g + 1) * gw]
            if not first:
                v = v + (vf_ref[rows, cols] - v) * _sigmoid(v0_ref[:, cols] + nu_pre[rows, cols])
            kk = kk0[rows, cols] * lax.rsqrt(jnp.maximum(ss[rows, cols], KK_EPS * KK_EPS))
            k = z_ref[rows, r_dim + g * gw:r_dim + (g + 1) * gw] * (ta * ka_half[:, cols] + ka_rest[:, cols])
            hi = logw.astype(BF16)
            hi_t[ci][g] = hi
            lo_t[ci][g] = (logw - hi.astype(F32)).astype(BF16)
            rkr_t[ci][g] = (z_ref[rows, cols] * k * rk_ref[:, cols]).astype(BF16)
            kk_half = 0.5 * kk
            keep[ci][g] = (logw, v, kk, k, kk_half * ta + kk_half)

    tri = tri_ref[...]
    cs_all = _dot(tri, assemble(hi_t)) + _dot(tri, assemble(lo_t))
    rkr = assemble(rkr_t)
    bsum = jnp.concatenate([_dot(rkr[:, q * V7X_MXU_DIM:(q + 1) * V7X_MXU_DIM], seg)
                            for q in range(r_dim // V7X_MXU_DIM)], axis=1)

    for ci in range(n_chunks):
        for g in range(n_groups):
            rows = slice(ci * c, (ci + 1) * c)
            cols = slice(g * gw, (g + 1) * gw)
            logw, v, kk, k, b = keep[ci][g]
            cs = cs_all[rows, cols]
            w_inv = jnp.exp(-cs)
            at_s[rows, cols] = (-kk * jnp.exp(cs - logw)).astype(BF16)
            rt_s[rows, cols] = (z_ref[rows, cols] * jnp.exp(cs)).astype(BF16)
            bt_s[rows, cols] = (b * w_inv).astype(BF16)
            kt_s[rows, cols] = (k * w_inv).astype(BF16)
            v_s[rows, cols] = v.astype(BF16)
            wc_s[ci:ci + 1, cols] = jnp.exp(cs[c - 1:c])
            bonus_s[rows, cols] = bsum[rows, cols] * v
    gate = z_ref[:, 3 * r_dim:4 * r_dim]

    lane = lax.broadcasted_iota(jnp.int32, (c, V7X_LANES), 1)
    t_row = lax.broadcasted_iota(jnp.int32, (c, V7X_LANES), 0)
    s_col = lane % head
    lo_half = lane < head
    hi_half = lane >= head
    strict_lo = lo_half & (s_col < t_row)
    strict_hi = hi_half & (s_col < t_row)
    incl = s_col <= t_row
    eye = (lax.broadcasted_iota(jnp.int32, (2 * c, 2 * c), 0)
           == lax.broadcasted_iota(jnp.int32, (2 * c, 2 * c), 1)).astype(F32)
    same_head = (lax.broadcasted_iota(jnp.int32, (V7X_LANES, V7X_LANES), 0) // head
                 == lax.broadcasted_iota(jnp.int32, (V7X_LANES, V7X_LANES), 1) // head)

    def tile(src, ci, p):
        return src[ci * c:(ci + 1) * c, p * V7X_LANES:(p + 1) * V7X_LANES]

    def head_rows(xv):
        zero = jnp.zeros_like(xv)
        return jnp.concatenate([jnp.where(lo_half, xv, zero), jnp.where(lo_half, zero, xv)], axis=0)

    lw = V7X_LANES

    def gram(q, ci, p):
        bt, kt = tile(bt_s, ci, p), tile(kt_s, ci, p)
        zero = jnp.zeros_like(bt)
        rhs = jnp.concatenate([jnp.where(lo_half, bt, zero), jnp.where(lo_half, kt, zero),
                               jnp.where(lo_half, zero, kt), jnp.where(lo_half, zero, bt)], axis=0)
        lhs = jnp.concatenate([tile(at_s, ci, p), tile(rt_s, ci, p)], axis=0)
        gm = _dot_nt(lhs, rhs)
        g0, g1 = gm[0:c, 0:lw], gm[0:c, lw:2 * lw]
        nmat = jnp.concatenate([jnp.where(strict_lo, g0, 0.0), jnp.where(strict_hi, g1, 0.0)], axis=0)
        aak_s[q] = jnp.concatenate([jnp.where(strict_hi, g0, 0.0), jnp.where(strict_lo, g1, 0.0)],
                                   axis=0).astype(BF16)
        ar_s[q, 0] = jnp.where(incl, gm[c:2 * c, 0:lw], 0.0).astype(BF16)
        ar_s[q, 1] = jnp.where(incl, gm[c:2 * c, lw:2 * lw], 0.0).astype(BF16)
        tinv = eye + nmat
        tinv_s[q] = tinv
        ch_s[q, :, 0:lw] = nmat.astype(BF16)
        ch_s[q, :, lw:2 * lw] = tinv.astype(BF16)

    def square(q):
        pw = ch_s[q, :, 0:lw]
        ch_s[q, :, 0:lw] = _dot(pw, pw).astype(BF16)

    def square_and_extend(q):
        res = _dot(ch_s[q, :, 0:lw], ch_s[q])
        tinv = tinv_s[q] + res[:, lw:2 * lw]
        tinv_s[q] = tinv
        ch_s[q, :, 0:lw] = res[:, 0:lw].astype(BF16)
        ch_s[q, :, lw:2 * lw] = tinv.astype(BF16)

    def extend_last(q):
        tinv = tinv_s[q] + _dot(ch_s[q, :, 0:lw], ch_s[q, :, lw:2 * lw])
        ch_s[q, :, lw:2 * lw] = tinv.astype(BF16)

    def local_x(q, ci, p):
        vt = tile(v_s, ci, p)
        xloc_s[q] = _dot(aak_s[q], jnp.concatenate([vt, vt], axis=0)).astype(BF16)

    def apply_inverse(q, ci, p):
        tu = _dot(ch_s[q, :, lw:2 * lw], jnp.concatenate([head_rows(tile(at_s, ci, p)), xloc_s[q]], axis=1))
        ta_s[q] = tu[:, 0:lw].astype(BF16)
        uloc_s[q] = tu[:, lw:2 * lw]

    def sweep_stages(chunks):
        qs = [(ci * n_pairs + p, ci, p) for ci in chunks for p in range(n_pairs)]
        stages = [[functools.partial(gram, q, ci, p) for q, ci, p in qs],
                  [functools.partial(square, q) for q, _, _ in qs]]
        stages += [[functools.partial(square_and_extend, q) for q, _, _ in qs] for _ in range(levels - 2)]
        stages += [[functools.partial(extend_last, q) for q, _, _ in qs],
                   [functools.partial(local_x, q, ci, p) for q, ci, p in qs],
                   [functools.partial(apply_inverse, q, ci, p) for q, ci, p in qs]]
        return stages

    def read_state(ci):
        for p in range(n_pairs):
            q = ci * n_pairs + p
            lhs = jnp.concatenate([ta_s[q], tile(rt_s, ci, p)], axis=0)
            xh = _dot_nt(lhs, state_ref[p].astype(BF16))
            u2 = xh[0:2 * c] + uloc_s[q]
            u_s[p] = jnp.where(lo_half, u2[0:c], u2[c:2 * c]).astype(BF16)
            yr_s[p] = xh[2 * c:3 * c]

    def write_state(ci):
        for p in range(n_pairs):
            q = ci * n_pairs + p
            ls = slice(p * lw, (p + 1) * lw)
            ut, vt = u_s[p], tile(v_s, ci, p)
            uv = jnp.concatenate([ut, vt], axis=0)
            upd = _dot_tn(uv, jnp.concatenate([tile(bt_s, ci, p), tile(kt_s, ci, p)], axis=0))
            state_ref[p] = (state_ref[p] + jnp.where(same_head, upd, 0.0)) * wc_s[ci:ci + 1, ls]
            y0 = _dot(ar_s[q, 0], uv)
            y1 = _dot(ar_s[q, 1], jnp.concatenate([vt, ut], axis=0))
            y_s[ci * c:(ci + 1) * c, ls] = yr_s[p] + jnp.where(lo_half, y0, y1)

    groups = [list(range(g0, min(g0 + SWEEP_GROUP, n_chunks))) for g0 in range(0, n_chunks, SWEEP_GROUP)]
    prev = []
    for grp in groups:
        stages = sweep_stages(grp)
        riders = [functools.partial(f, ci) for ci in prev for f in (read_state, write_state)]
        done = 0
        for si, stage in enumerate(stages):
            for task in stage:
                task()
            want = (si + 1) * len(riders) // len(stages)
            for rider in riders[done:want]:
                rider()
            done = want
        prev = grp
    for ci in prev:
        read_state(ci)
        write_state(ci)

    y = y_s[...]
    segn = segn_ref[...]
    yc = y - _seg_sum(y, segn)
    yn = yc * lax.rsqrt(_seg_sum(yc * yc, segn) + GN_EPS) * lnw_ref[...] + lnb_ref[...]
    y_ref[...] = (yn + bonus_s[...]) * gate


def _rwkv_constants(tb, head):
    c = CHUNK
    i = np.arange(tb)
    tri = ((i[:, None] // c) == (i[None, :] // c)) & (i[None, :] <= i[:, None])
    lanes = np.arange(V7X_MXU_DIM) // head
    seg = (lanes[:, None] == lanes[None, :]).astype(np.float32)
    return jnp.asarray(seg, BF16), jnp.asarray(seg / head, BF16), jnp.asarray(tri, BF16)


def _rwkv(z, z_l, vfirst, prm, *, batch, seq, tb, r_dim, heads, head, first):
    m = batch * seq
    nt = seq // tb
    d_lora = prm["w_up"].shape[0]
    l_dim = d_lora + prm["a_up"].shape[0]
    n_pairs = heads // 2
    seg, segn, tri = _rwkv_constants(tb, head)
    n_prob = (tb // CHUNK) * n_pairs
    c2 = 2 * CHUNK

    def rows(width, col):
        return pl.BlockSpec((tb, width), lambda bi, ti: (bi * nt + ti, col))

    def whole(arr):
        return pl.BlockSpec(arr.shape, lambda bi, ti: (0,) * arr.ndim, pipeline_mode=pl.Buffered(1))

    ops = [z, z_l]
    specs = [rows(4 * r_dim, 0), rows(l_dim, 0)]
    if not first:
        ops.append(vfirst)
        specs.append(rows(r_dim, 0))
    names = ["w0", "w_up", "a0", "a_up"]
    if not first:
        names += ["v0", "v_down", "v_up"]
    names += ["k_k", "k_a", "r_k", "ln_w", "ln_b"]
    consts = [prm[nm] for nm in names] + [seg, segn, tri]
    ops += consts
    specs += [whole(a) for a in consts]

    out_shape = [jax.ShapeDtypeStruct((m, r_dim), F32)]
    out_specs = [rows(r_dim, 0)]
    if first:
        out_shape.append(jax.ShapeDtypeStruct((m, r_dim), F32))
        out_specs.append(rows(r_dim, 0))

    scratch = [
        pltpu.VMEM((n_pairs, V7X_LANES, V7X_LANES), F32),
    ] + [pltpu.VMEM((tb, r_dim), BF16) for _ in range(5)] + [
        pltpu.VMEM((8, r_dim), F32),
        pltpu.VMEM((tb, r_dim), F32),
        pltpu.VMEM((tb, r_dim), F32),
        pltpu.VMEM((n_prob, c2, 2 * c2), BF16),
        pltpu.VMEM((n_prob, c2, c2), F32),
        pltpu.VMEM((n_prob, c2, c2), BF16),
        pltpu.VMEM((n_prob, 2, CHUNK, c2), BF16),
        pltpu.VMEM((n_prob, c2, V7X_LANES), BF16),
        pltpu.VMEM((n_prob, c2, V7X_LANES), BF16),
        pltpu.VMEM((n_prob, c2, V7X_LANES), F32),
        pltpu.VMEM((n_pairs, CHUNK, V7X_LANES), BF16),
        pltpu.VMEM((n_pairs, CHUNK, V7X_LANES), F32),
    ]
    res = pl.pallas_call(
        functools.partial(_rwkv_kernel, first=first, n_pairs=n_pairs, head=head, d_lora=d_lora),
        grid=(batch, nt),
        in_specs=specs,
        out_specs=out_specs,
        out_shape=out_shape,
        scratch_shapes=scratch,
        compiler_params=pltpu.CompilerParams(
            dimension_semantics=("parallel", "arbitrary"), vmem_limit_bytes=V7X_VMEM_LIMIT_BYTES),
        name="rwkv_first" if first else "rwkv",
    )(*ops)
    return (res[0], res[1]) if first else (res[0], vfirst)


def _pool_mix(u, gp, carry, pos0, wp_ref, ps_ref):
    tb, p_dim = u.shape
    cg = p_dim // len(POOL_WINDOWS)
    ext = jnp.concatenate([carry, u], axis=0)
    sums = {}
    s = ext
    span = 1
    while span < MAX_POOL_WINDOW:
        s = s + pltpu.roll(s, span, 0)
        span *= 2
        sums[span] = s
    pos = (pos0 + 1 + lax.broadcasted_iota(jnp.int32, (tb, 1), 0)).astype(F32)
    outs = []
    for gi, win in enumerate(POOL_WINDOWS):
        cols = slice(gi * cg, (gi + 1) * cg)
        mean = sums[win][MAX_POOL_WINDOW:, cols] / jnp.minimum(pos, float(win))
        d = mean - u[:, cols]
        outs.append(_dot(d.astype(BF16), wp_ref[gi]))
    return jnp.concatenate(outs, axis=1) * ps_ref[...] * gp


def _out_kernel(x_ref, yr_ref, zp_ref, p_ref, wo_ref, wp_ref, ps_ref, wpg_ref, bpg_ref, wple_ref, ng_ref, *rest,
                last, blocks_per_seq):
    if last:
        o_ref, carry_ref = rest
    else:
        o_ref, h_ref, carry_ref = rest
    i = pl.program_id(0)
    tm = x_ref.shape[0]
    r_dim = yr_ref.shape[1]
    p_dim = ps_ref.shape[1]

    @pl.when(i == 0)
    def _():
        carry_ref[...] = jnp.zeros_like(carry_ref)

    keep = (i % blocks_per_seq != 0).astype(F32)
    x1 = x_ref[...] + _dot(yr_ref[...].astype(BF16), wo_ref[0:r_dim, :])
    yp = _pool_mix(zp_ref[:, 0:p_dim], zp_ref[:, p_dim:2 * p_dim], carry_ref[...] * keep,
                   (i % blocks_per_seq) * tm, wp_ref, ps_ref)
    carry_ref[...] = zp_ref[tm - MAX_POOL_WINDOW:tm, 0:p_dim]
    x1 = x1 + _dot(yp.astype(BF16), wo_ref[r_dim:, :])
    gate = _sigmoid(_dot(x1.astype(BF16), wpg_ref[...]) + bpg_ref[...])
    x2 = x1 + gate * _dot(p_ref[...].astype(BF16), wple_ref[...])
    ms = jnp.mean(x2 * x2, axis=-1, keepdims=True)
    normed = x2 * lax.rsqrt(ms + RMS_EPS) * ng_ref[...]
    if last:
        o_ref[...] = normed
    else:
        o_ref[...] = x2
        h_ref[...] = normed.astype(BF16)


def _out(x2, yr, z, p3, layer, w_out, w_pool, pool_scale, w_pg, b_pg, w_ple, next_g, *, tm, seq, last):
    m, d = x2.shape
    r_dim = yr.shape[1]
    pool_dim = pool_scale.shape[1]

    def whole(arr):
        return pl.BlockSpec(arr.shape, lambda i: (0,) * arr.ndim, pipeline_mode=pl.Buffered(1))

    return pl.pallas_call(
        functools.partial(_out_kernel, last=last, blocks_per_seq=seq // tm),
        grid=(m // tm,),
        in_specs=[
            pl.BlockSpec((tm, d), lambda i: (i, 0)),
            pl.BlockSpec((tm, r_dim), lambda i: (i, 0)),
            pl.BlockSpec((tm, 2 * pool_dim), lambda i: (i, (4 * r_dim) // (2 * pool_dim))),
            pl.BlockSpec((None, tm, p3.shape[2]), lambda i: (layer, i, 0)),
            whole(w_out), whole(w_pool), whole(pool_scale), whole(w_pg), whole(b_pg), whole(w_ple),
            whole(next_g),
        ],
        out_specs=[pl.BlockSpec((tm, d), lambda i: (i, 0))] * (1 if last else 2),
        out_shape=[jax.ShapeDtypeStruct((m, d), F32)] + ([] if last else [jax.ShapeDtypeStruct((m, d), BF16)]),
        scratch_shapes=[pltpu.VMEM((MAX_POOL_WINDOW, pool_dim), F32)],
        compiler_params=pltpu.CompilerParams(
            dimension_semantics=("arbitrary",), vmem_limit_bytes=V7X_VMEM_LIMIT_BYTES),
        name="out_proj",
    )(x2, yr, z, p3, w_out, w_pool, pool_scale, w_pg, b_pg, w_ple, next_g)


def _tile(n, cap):
    t = min(n, cap)
    assert n % t == 0
    return t


def kernel(x, p, norm_g, w_in, mu, w0, w_up, a0, a_up, v0, v_down, v_up, k_k, k_a, r_k, ln_w, ln_b, w_pool,
           pool_scale, w_out, w_ple, w_pg, b_pg, final_g):
    batch, seq, d = x.shape
    depth = w_in.shape[0]
    r_dim = w0.shape[1]
    heads, head = r_k.shape[1], r_k.shape[2]
    pool_dim = pool_scale.shape[1]
    n_shift = mu.shape[1]
    l_dim = n_shift - 3 * r_dim
    m = batch * seq
    assert 2 * head == V7X_LANES and heads % 2 == 0 and heads * head == r_dim and r_dim % V7X_MXU_DIM == 0
    assert l_dim % V7X_LANES == 0
    assert pool_dim == r_dim and pool_dim % len(POOL_WINDOWS) == 0

    tb_rwkv = _tile(seq, 256)
    tm_in = _tile(seq, 1024)
    tm_in_h = _tile(seq, 2048)
    tm_out = _tile(seq, 256)
    assert tb_rwkv % CHUNK == 0 and tb_rwkv // CHUNK <= 8 and tm_out >= MAX_POOL_WINDOW

    w_in_p = jnp.concatenate(
        [w_in[..., :3 * r_dim], w_in[..., n_shift:n_shift + r_dim], w_in[..., n_shift + r_dim:]],
        axis=-1).astype(BF16)
    w_in_l = w_in[..., 3 * r_dim:n_shift].astype(BF16)
    w_out_b = w_out.astype(BF16)
    w_pg_b = w_pg.astype(BF16)
    w_ple_b = w_ple.astype(BF16)
    w_pool_b = w_pool.astype(BF16)
    p3 = p.reshape(depth, m, p.shape[-1])
    x2 = x.reshape(m, d)
    fg = final_g.reshape(1, d)

    vfirst = None
    for i in range(depth):
        z, z_l = _in_proj(x2 if i == 0 else h, norm_g[i].reshape(1, d), w_in_p[i], w_in_l[i],
                          mu[i, :3 * r_dim].reshape(1, -1), mu[i, 3 * r_dim:].reshape(1, -1),
                          tm=tm_in if i == 0 else tm_in_h, tn=r_dim, seq=seq, n_shift_tiles=3, gate_tiles=(3, 5),
                          normalized=(i > 0))
        prm = {
            "w0": w0[i].reshape(1, -1), "w_up": w_up[i].astype(BF16),
            "a0": a0[i].reshape(1, -1), "a_up": a_up[i].astype(BF16),
            "k_k": k_k[i].reshape(1, -1), "k_a": k_a[i].reshape(1, -1), "r_k": r_k[i].reshape(1, -1),
            "ln_w": ln_w[i].reshape(1, -1), "ln_b": ln_b[i].reshape(1, -1),
        }
        if i > 0:
            prm.update({"v0": v0[i - 1].reshape(1, -1), "v_down": v_down[i - 1].astype(BF16),
                        "v_up": v_up[i - 1].astype(BF16)})
        yr, vfirst = _rwkv(z, z_l, vfirst, prm, batch=batch, seq=seq, tb=tb_rwkv, r_dim=r_dim,
                           heads=heads, head=head, first=(i == 0))
        last = i == depth - 1
        res = _out(x2, yr, z, p3, i, w_out_b[i], w_pool_b[i], pool_scale[i].reshape(1, -1), w_pg_b[i],
                   b_pg[i].reshape(1, d), w_ple_b[i], fg if last else norm_g[i + 1].reshape(1, d), tm=tm_out,
                   seq=seq, last=last)
        x2, h = (res[0], None) if last else res
    return x2.reshape(batch, seq, d)
```
